```python
import math
import jax, jax.numpy as jnp
from jax import lax
import numpy as np

D_MODEL = 1024
BATCH = 4
SEQ = 4096
DEPTH = 4
DEC_BATCH = 128
DEC_SEQ = 1
PAST_LEN = 2048
PAGE_SIZE = 128

HEAD_DIM = 64
MEM_HEADS = 4
MEM_WIDTH = MEM_HEADS * HEAD_DIM
N_MEM = 256
MIX_WIDTH = D_MODEL
MIXER_WIDTH = MIX_WIDTH - MEM_WIDTH
N_MIXERS = 3
NSA_HEADS = MIXER_WIDTH // HEAD_DIM
NSA_GROUPS = 2
NSA_HPG = NSA_HEADS // NSA_GROUPS
NSA_BLOCK = 64
NSA_TOPK = 16
NSA_WINDOW = 512
CONV_CH = MIXER_WIDTH
CONV_WIDTH = 31
SB_HEADS = MIXER_WIDTH // HEAD_DIM
FFN_HIDDEN = -(-8 * D_MODEL // (3 * 256)) * 256
ROPE_THETA = 500000.0
ROT_DIM = HEAD_DIM // 4
QBLOCK = 128
EPS = 1e-6
N_NSA_LAYERS = (DEPTH + N_MIXERS - 1) // N_MIXERS
N_CONV_LAYERS = (DEPTH + N_MIXERS - 2) // N_MIXERS
N_SB_LAYERS = (DEPTH + N_MIXERS - 3) // N_MIXERS
NSA_IN = NSA_HEADS * HEAD_DIM + 6 * NSA_GROUPS * HEAD_DIM + 3 * NSA_HEADS
CONV_IN = 2 * CONV_CH
SB_IN = 3 * SB_HEADS * HEAD_DIM

kernel_name = 'nsa_conformer_stickbreak_hybrid_step'


def rms_norm(x, g):
    xf = x.astype(jnp.float32)
    y = xf * lax.rsqrt(jnp.mean(xf * xf, axis=-1, keepdims=True) + EPS)
    return (y * g.astype(jnp.float32)).astype(x.dtype)


def layer_norm(x, g, b):
    xf = x.astype(jnp.float32)
    mu = jnp.mean(xf, axis=-1, keepdims=True)
    var = jnp.mean(jnp.square(xf - mu), axis=-1, keepdims=True)
    y = (xf - mu) * lax.rsqrt(var + EPS) * g.astype(jnp.float32) + b.astype(jnp.float32)
    return y.astype(x.dtype)


def rope_partial(x, pos):
    half = ROT_DIM // 2
    inv = jnp.exp(-math.log(ROPE_THETA) * jnp.arange(0, ROT_DIM, 2, dtype=jnp.float32) / ROT_DIM)
    ang = pos.astype(jnp.float32)[:, None] * inv[None, :]
    cos = jnp.cos(ang)[None, :, None, :]
    sin = jnp.sin(ang)[None, :, None, :]
    xr = x[..., :ROT_DIM].astype(jnp.float32)
    x1, x2 = xr[..., :half], xr[..., half:]
    rot = jnp.concatenate([x1 * cos - x2 * sin, x2 * cos + x1 * sin], axis=-1)
    return jnp.concatenate([rot.astype(x.dtype), x[..., ROT_DIM:]], axis=-1)


def masked_softmax(s, mask):
    s = jnp.where(mask, s, -jnp.inf)
    m = jnp.max(s, axis=-1, keepdims=True)
    m = jnp.where(jnp.isfinite(m), m, 0.0)
    p = jnp.exp(s - m)
    return p / jnp.maximum(jnp.sum(p, axis=-1, keepdims=True), 1e-30)


def sweep_query_blocks(fn, qb, *per_query):
    bsz, t = per_query[0].shape[:2]
    nblk = -(-t // qb)
    tp = nblk * qb

    def split_blocks(a):
        a = jnp.pad(a, [(0, 0), (0, tp - t)] + [(0, 0)] * (a.ndim - 2))
        return jnp.moveaxis(a.reshape((bsz, nblk, qb) + a.shape[2:]), 1, 0)

    xs = tuple(split_blocks(a) for a in per_query)
    out = lax.map(lambda args: fn(args[0], *args[1:]), (jnp.arange(nblk),) + xs)
    out = jnp.moveaxis(out, 0, 1)
    out = out.reshape((bsz, tp) + out.shape[3:])
    return out[:, :t]


def nsa_attend(q, kv_cs, kv_win, gates, q0, w_cmp):
    bsz, t_q = q.shape[:2]
    seq_len = kv_cs.shape[1]
    n_blk = -(-seq_len // NSA_BLOCK)
    kv_cs = jnp.pad(kv_cs, [(0, 0), (0, n_blk * NSA_BLOCK - seq_len)] + [(0, 0)] * 3)
    blocks = kv_cs.reshape(bsz, n_blk, NSA_BLOCK, 4, NSA_GROUPS, HEAD_DIM)
    cmp_in = blocks[:, :, :, 0:2].transpose(0, 1, 3, 4, 2, 5).reshape(bsz, n_blk, 2, NSA_GROUPS, NSA_BLOCK * HEAD_DIM)
    kv_cmp = jnp.einsum('bnkgf,kfd->bnkgd', cmp_in, w_cmp).astype(jnp.float32)
    k_cmp, v_cmp = kv_cmp[:, :, 0], kv_cmp[:, :, 1]
    sel = blocks[:, :, :, 2:4].transpose(0, 3, 4, 1, 2, 5)
    k_sel, v_sel = sel[:, 0], sel[:, 1]
    n_sel = min(NSA_TOPK, n_blk)
    kw0 = seq_len - kv_win.shape[1]
    qb = min(QBLOCK, t_q)
    span = NSA_WINDOW + qb - 1
    kv_win = jnp.pad(kv_win, [(0, 0), (NSA_WINDOW, qb)] + [(0, 0)] * 3)
    scale = HEAD_DIM ** -0.5
    blk_ids = jnp.arange(n_blk)
    gather_blocks = jax.vmap(jax.vmap(lambda kb, ix: kb[ix]))

    def block_fn(i, q_blk, g_blk):
        t0 = q0 + i * qb
        t = t0 + jnp.arange(qb)
        qg = q_blk.reshape(bsz, qb, NSA_GROUPS, NSA_HPG, HEAD_DIM).astype(jnp.float32)
        s = jnp.einsum('btghd,bngd->bghtn', qg, k_cmp) * scale
        complete = (blk_ids[None, :] + 1) * NSA_BLOCK <= t[:, None] + 1
        p_cmp = masked_softmax(s, complete)
        o_cmp = jnp.einsum('bghtn,bngd->btghd', p_cmp, v_cmp)
        cur = t[:, None] // NSA_BLOCK
        imp = jnp.where(blk_ids[None, :] == cur, jnp.inf,
                        jnp.where(blk_ids[None, :] < cur, p_cmp.sum(axis=2), -jnp.inf))
        top_val, top_idx = lax.top_k(imp, n_sel)
        k_g = gather_blocks(k_sel, top_idx).astype(jnp.float32).reshape(bsz, NSA_GROUPS, qb, n_sel * NSA_BLOCK, HEAD_DIM)
        v_g = gather_blocks(v_sel, top_idx).astype(jnp.float32).reshape(bsz, NSA_GROUPS, qb, n_sel * NSA_BLOCK, HEAD_DIM)
        k_pos = top_idx[..., None] * NSA_BLOCK + jnp.arange(NSA_BLOCK)
        sel_ok = (top_val[..., None] > -jnp.inf) & (k_pos <= t[:, None, None])
        sel_ok = sel_ok.reshape(bsz, NSA_GROUPS, 1, qb, n_sel * NSA_BLOCK)
        s = jnp.einsum('btghd,bgtmd->bghtm', qg, k_g) * scale
        p = masked_softmax(s, sel_ok)
        o_slc = jnp.einsum('bghtm,bgtmd->btghd', p, v_g)
        kw = lax.dynamic_slice_in_dim(kv_win, t0 - kw0 + 1, span, axis=1).astype(jnp.float32)
        w_pos = t0 - NSA_WINDOW + 1 + jnp.arange(span)
        w_ok = (w_pos[None, :] <= t[:, None]) & (w_pos[None, :] > t[:, None] - NSA_WINDOW) & (w_pos[None, :] >= kw0)
        s = jnp.einsum('btghd,bsgd->bghts', qg, kw[:, :, 0]) * scale
        p = masked_softmax(s, w_ok)
        o_win = jnp.einsum('bghts,bsgd->btghd', p, kw[:, :, 1])
        g = g_blk.astype(jnp.float32).reshape(bsz, qb, NSA_GROUPS, NSA_HPG, 3)
        o = g[..., 0:1] * o_cmp + g[..., 1:2] * o_slc + g[..., 2:3] * o_win
        return o.reshape(bsz, qb, NSA_HEADS * HEAD_DIM).astype(q_blk.dtype)

    return sweep_query_blocks(block_fn, qb, q, gates)


def stick_breaking_attend(q, kv, q0):
    bsz, t_q = q.shape[:2]
    k = kv[:, :, 0].astype(jnp.float32)
    v = kv[:, :, 1].astype(jnp.float32)
    k_pos = jnp.arange(kv.shape[1])
    qb = min(QBLOCK, t_q)
    scale = HEAD_DIM ** -0.5

    def block_fn(i, q_blk):
        t = q0 + i * qb + jnp.arange(qb)
        z = jnp.einsum('bthd,bshd->bhts', q_blk.astype(jnp.float32), k) * scale
        before = k_pos[None, :] < t[:, None]
        log_stay = jnp.where(before, jax.nn.log_sigmoid(-z), 0.0)
        log_later = lax.cumsum(log_stay, axis=3, reverse=True) - log_stay
        a = jnp.where(before, jnp.exp(jax.nn.log_sigmoid(z) + log_later), 0.0)
        o = jnp.einsum('bhts,bshd->bthd', a, v)
        return o.reshape(bsz, qb, SB_HEADS * HEAD_DIM).astype(q_blk.dtype)

    return sweep_query_blocks(block_fn, qb, q)


def memory_kv(mem, g_norm, w_kv, g_k):
    bsz, m = mem.shape[:2]
    kv = (rms_norm(mem, g_norm) @ w_kv).reshape(bsz, m, 2, MEM_HEADS, HEAD_DIM)
    return jnp.stack([rms_norm(kv[:, :, 0], g_k), kv[:, :, 1]], axis=2)


def memory_attend(q, mem_kv):
    s = jnp.einsum('bthd,bmhd->bhtm', q.astype(jnp.float32), mem_kv[:, :, 0].astype(jnp.float32)) * HEAD_DIM ** -0.5
    p = jax.nn.softmax(s, axis=-1)
    o = jnp.einsum('bhtm,bmhd->bthd', p, mem_kv[:, :, 1].astype(jnp.float32))
    return o.astype(q.dtype)


def nsa_mixer(u, pos, past_cs, past_win, q0, qk_g, w_cmp):
    bsz, t = u.shape[:2]
    nq = NSA_HEADS * HEAD_DIM
    nkv = 6 * NSA_GROUPS * HEAD_DIM
    q = rope_partial(rms_norm(u[..., :nq].reshape(bsz, t, NSA_HEADS, HEAD_DIM), qk_g[0]), pos)
    kv = u[..., nq:nq + nkv].reshape(bsz, t, 6, NSA_GROUPS, HEAD_DIM)
    gates = jax.nn.sigmoid(u[..., nq + nkv:].reshape(bsz, t, NSA_HEADS, 3))
    k_c = rope_partial(rms_norm(kv[:, :, 0], qk_g[1]), pos)
    k_s = rope_partial(rms_norm(kv[:, :, 2], qk_g[2]), pos)
    k_w = rope_partial(rms_norm(kv[:, :, 4], qk_g[3]), pos)
    new_cs = jnp.stack([k_c, kv[:, :, 1], k_s, kv[:, :, 3]], axis=2)
    new_win = jnp.stack([k_w, kv[:, :, 5]], axis=2)
    full_cs = new_cs if past_cs is None else jnp.concatenate([past_cs, new_cs], axis=1)
    full_win = new_win if past_win is None else jnp.concatenate([past_win, new_win], axis=1)
    o = nsa_attend(q, full_cs, full_win, gates, q0, w_cmp)
    return o, new_cs, full_win


def conv_mixer(u, past_buf, w, b, ln_g, ln_b):
    bsz, t = u.shape[:2]
    glu = u[..., :CONV_CH] * jax.nn.sigmoid(u[..., CONV_CH:])
    if past_buf is None:
        past_buf = jnp.zeros((bsz, CONV_WIDTH - 1, CONV_CH), glu.dtype)
    xin = jnp.concatenate([past_buf, glu], axis=1)
    y = lax.conv_general_dilated(xin, w[:, None, :], window_strides=(1,), padding='VALID',
                                 dimension_numbers=('NWC', 'WIO', 'NWC'), feature_group_count=CONV_CH) + b
    y = layer_norm(y, ln_g, ln_b)
    return y * jax.nn.sigmoid(y), xin[:, -(CONV_WIDTH - 1):]


def sb_mixer(u, past_kv, q0):
    bsz, t = u.shape[:2]
    qkv = u.reshape(bsz, t, 3, SB_HEADS, HEAD_DIM)
    new_kv = qkv[:, :, 1:3]
    full = new_kv if past_kv is None else jnp.concatenate([past_kv, new_kv], axis=1)
    return stick_breaking_attend(qkv[:, :, 0], full, q0), new_kv


def swiglu(h, w_in, w_out):
    gu = h @ w_in
    return (jax.nn.silu(gu[..., :FFN_HIDDEN]) * gu[..., FFN_HIDDEN:]) @ w_out


def setup_inputs(seed: int = 0) -> dict:
    key = jax.random.key(seed)
    keys = jax.random.split(key, 32)

    def nrm(k, shape, scale):
        return jax.random.normal(k, shape, jnp.float32) * scale

    n_pages = PAST_LEN // PAGE_SIZE
    n_pool = (DEC_BATCH * n_pages * 5 + 3) // 4
    win_buf = min(NSA_WINDOW, PAST_LEN)
    page_table = jax.random.permutation(keys[7], n_pool)[:DEC_BATCH * n_pages].reshape(DEC_BATCH, n_pages).astype(jnp.int32)
    return {
        'x_prompt': nrm(keys[0], (BATCH, SEQ, D_MODEL), 1.0),
        'x_sample': nrm(keys[1], (DEC_BATCH, DEC_SEQ, D_MODEL), 1.0),
        'cache_nsa_kv': nrm(keys[2], (N_NSA_LAYERS, n_pool, PAGE_SIZE, 4, NSA_GROUPS, HEAD_DIM), 1.0),
        'state_nsa_win': nrm(keys[3], (N_NSA_LAYERS, DEC_BATCH, win_buf, 2, NSA_GROUPS, HEAD_DIM), 1.0),
        'state_conv': nrm(keys[4], (N_CONV_LAYERS, DEC_BATCH, CONV_WIDTH - 1, CONV_CH), 0.5),
        'cache_sb_kv': nrm(keys[5], (N_SB_LAYERS, n_pool, PAGE_SIZE, 2, SB_HEADS, HEAD_DIM), 1.0),
        'cache_mem_kv': nrm(keys[6], (DEPTH, DEC_BATCH, N_MEM, 2, MEM_HEADS, HEAD_DIM), 1.0),
        'page_table': page_table,
        'mem_prompt': nrm(keys[8], (BATCH, N_MEM, D_MODEL), 1.0),
        'norm_mix': 1.0 + nrm(keys[9], (DEPTH, D_MODEL), 0.02),
        'norm_ffn': 1.0 + nrm(keys[10], (DEPTH, D_MODEL), 0.02),
        'norm_mem': 1.0 + nrm(keys[11], (DEPTH, D_MODEL), 0.02),
        'w_in_nsa': nrm(keys[12], (N_NSA_LAYERS, D_MODEL, NSA_IN + MEM_WIDTH), D_MODEL ** -0.5),
        'w_in_conv': nrm(keys[13], (N_CONV_LAYERS, D_MODEL, CONV_IN + MEM_WIDTH), D_MODEL ** -0.5),
        'w_in_sb': nrm(keys[14], (N_SB_LAYERS, D_MODEL, SB_IN + MEM_WIDTH), D_MODEL ** -0.5),
        'w_out': nrm(keys[15], (DEPTH, MIX_WIDTH, D_MODEL), MIX_WIDTH ** -0.5),
        'w_mem_kv': nrm(keys[16], (DEPTH, D_MODEL, 2 * MEM_WIDTH), D_MODEL ** -0.5),
        'qk_norm_nsa': 1.0 + nrm(keys[17], (N_NSA_LAYERS, 4, HEAD_DIM), 0.02),
        'qk_norm_mem': 1.0 + nrm(keys[18], (DEPTH, 2, HEAD_DIM), 0.02),
        'w_nsa_cmp': nrm(keys[19], (N_NSA_LAYERS, 2, NSA_BLOCK * HEAD_DIM, HEAD_DIM), (NSA_BLOCK * HEAD_DIM) ** -0.5),
        'conv_w': nrm(keys[20], (N_CONV_LAYERS, CONV_WIDTH, CONV_CH), CONV_WIDTH ** -0.5),
        'conv_b': nrm(keys[21], (N_CONV_LAYERS, CONV_CH), 0.02),
        'conv_ln_g': 1.0 + nrm(keys[22], (N_CONV_LAYERS, CONV_CH), 0.02),
        'conv_ln_b': nrm(keys[23], (N_CONV_LAYERS, CONV_CH), 0.02),
        'w_ffn_in': nrm(keys[24], (DEPTH, D_MODEL, 2 * FFN_HIDDEN), D_MODEL ** -0.5),
        'w_ffn_out': nrm(keys[25], (DEPTH, FFN_HIDDEN, D_MODEL), FFN_HIDDEN ** -0.5),
    }


def reference(x_prompt, x_sample, cache_nsa_kv, state_nsa_win, state_conv, cache_sb_kv, cache_mem_kv,
              page_table, mem_prompt, norm_mix, norm_ffn, norm_mem, w_in_nsa, w_in_conv, w_in_sb, w_out,
              w_mem_kv, qk_norm_nsa, qk_norm_mem, w_nsa_cmp, conv_w, conv_b, conv_ln_g, conv_ln_b,
              w_ffn_in, w_ffn_out):
    n_seq, n_pages = page_table.shape
    past = n_pages * PAGE_SIZE
    pos_p = jnp.arange(x_prompt.shape[1])
    pos_s = past + jnp.arange(x_sample.shape[1])

    def gather_pages(pool):
        rows = pool[page_table]
        return rows.reshape((n_seq, past) + pool.shape[2:])

    def token_mix(layer, h, pos, q0, mem_kv, past_state):
        kind, j = layer % N_MIXERS, layer // N_MIXERS
        bsz, t = h.shape[:2]
        if kind == 0:
            u = h @ w_in_nsa[j]
            past_cs, past_win = (None, None) if past_state is None else past_state
            o, new_cs, full_win = nsa_mixer(u[..., :NSA_IN], pos, past_cs, past_win, q0, qk_norm_nsa[j], w_nsa_cmp[j])
            keep = min(NSA_WINDOW, t) if past_win is None else past_win.shape[1]
            new_state = (new_cs, full_win[:, -keep:])
        elif kind == 1:
            u = h @ w_in_conv[j]
            o, new_buf = conv_mixer(u[..., :CONV_IN], past_state, conv_w[j], conv_b[j], conv_ln_g[j], conv_ln_b[j])
            new_state = new_buf
        else:
            u = h @ w_in_sb[j]
            o, new_kv = sb_mixer(u[..., :SB_IN], past_state, q0)
            new_state = new_kv
        qm = rms_norm(u[..., -MEM_WIDTH:].reshape(bsz, t, MEM_HEADS, HEAD_DIM), qk_norm_mem[layer, 0])
        om = memory_attend(qm, mem_kv).reshape(bsz, t, MEM_WIDTH)
        y = jnp.concatenate([o, om], axis=-1) @ w_out[layer]
        return y, new_state

    xp, xs = x_prompt, x_sample
    nsa_p, nsa_s, win_p, win_s, conv_p, conv_s, sb_p, sb_s, mem_p = [], [], [], [], [], [], [], [], []
    for layer in range(DEPTH):
        kind, j = layer % N_MIXERS, layer // N_MIXERS
        mem_kv_p = memory_kv(mem_prompt, norm_mem[layer], w_mem_kv[layer], qk_norm_mem[layer, 1])
        mem_p.append(mem_kv_p)
        if kind == 0:
            past_s = (gather_pages(cache_nsa_kv[j]), state_nsa_win[j])
        elif kind == 1:
            past_s = state_conv[j]
        else:
            past_s = gather_pages(cache_sb_kv[j])
        yp, st_p = token_mix(layer, rms_norm(xp, norm_mix[layer]), pos_p, 0, mem_kv_p, None)
        ys, st_s = token_mix(layer, rms_norm(xs, norm_mix[layer]), pos_s, past, cache_mem_kv[layer], past_s)
        xp = xp + yp
        xs = xs + ys
        xp = xp + swiglu(rms_norm(xp, norm_ffn[layer]), w_ffn_in[layer], w_ffn_out[layer])
        xs = xs + swiglu(rms_norm(xs, norm_ffn[layer]), w_ffn_in[layer], w_ffn_out[layer])
        if kind == 0:
            nsa_p.append(st_p[0]); win_p.append(st_p[1])
            nsa_s.append(st_s[0]); win_s.append(st_s[1])
        elif kind == 1:
            conv_p.append(st_p); conv_s.append(st_s)
        else:
            sb_p.append(st_p); sb_s.append(st_s)
    return (xp, xs, jnp.stack(nsa_p), jnp.stack(nsa_s), jnp.stack(win_p), jnp.stack(win_s),
            jnp.stack(conv_p), jnp.stack(conv_s), jnp.stack(sb_p), jnp.stack(sb_s), jnp.stack(mem_p))
```

```python
import functools
import math

import numpy as np
import jax
import jax.numpy as jnp
from jax import lax
from jax.experimental import pallas as pl
from jax.experimental.pallas import tpu as pltpu

F32 = jnp.float32
BF16 = jnp.bfloat16

HEAD_DIM = 64
MEM_HEADS = 4
MEM_WIDTH = MEM_HEADS * HEAD_DIM
N_MIXERS = 3
NSA_HEADS = 12
NSA_GROUPS = 2
NSA_HPG = NSA_HEADS // NSA_GROUPS
NSA_BLOCK = 64
NSA_TOPK = 16
NSA_WINDOW = 512
MIXER_WIDTH = NSA_HEADS * HEAD_DIM
CONV_WIDTH = 31
ROPE_THETA = 500000.0
ROT_DIM = HEAD_DIM // 4
EPS = 1e-6
PAGE_SIZE = 128

LANES = 128
SUBLANES = 8
VMEM_LIMIT = 56 * 1024 * 1024

ROW_TILE = 512
FFN_CHUNK = 256
Q_TILE = 128
SEL_CHUNK = 512
SB_CHUNK = 128
CONV_TILE = 512
CMP_BLOCKS = 256
CONV_HALO = 32

SCORE_SCALE = HEAD_DIM ** -0.5
BIG = 1e30


def _params(*sem):
    return pltpu.CompilerParams(dimension_semantics=sem, vmem_limit_bytes=VMEM_LIMIT)


def _iota(shape, dim):
    return lax.broadcasted_iota(jnp.int32, shape, dim)


def _dot(a, b):
    return jnp.dot(a.astype(BF16), b.astype(BF16), preferred_element_type=F32)


def _dot_nt(a, b):
    return lax.dot_general(a.astype(BF16), b.astype(BF16), (((1,), (1,)), ((), ())),
                           preferred_element_type=F32)


def _split2(x):
    hi = x.astype(BF16)
    lo = (x - hi.astype(F32)).astype(BF16)
    return hi, lo


def _split3(x):
    hi = x.astype(BF16)
    r = x - hi.astype(F32)
    mid = r.astype(BF16)
    lo = (r - mid.astype(F32)).astype(BF16)
    return hi, mid, lo


def _dot_x2(x, w_bf16):
    hi, lo = _split2(x)
    return (jnp.dot(hi, w_bf16, preferred_element_type=F32)
            + jnp.dot(lo, w_bf16, preferred_element_type=F32))


def _dot_x3(x, w_bf16):
    hi, mid, lo = _split3(x)
    return (jnp.dot(hi, w_bf16, preferred_element_type=F32)
            + jnp.dot(mid, w_bf16, preferred_element_type=F32)
            + jnp.dot(lo, w_bf16, preferred_element_type=F32))


def _dot_nt_precise(a, b):
    ah, al = _split2(a)
    bh, bl = _split2(b)
    dn = (((1,), (1,)), ((), ()))
    return (lax.dot_general(ah, bh, dn, preferred_element_type=F32)
            + lax.dot_general(ah, bl, dn, preferred_element_type=F32)
            + lax.dot_general(al, bh, dn, preferred_element_type=F32))


def _rms_rows(x, g):
    ms = jnp.mean(x * x, axis=-1, keepdims=True)
    return x * lax.rsqrt(ms + EPS) * g


def _head_rms(x, seg_ones, g):
    ms = _dot_x2(x * x, seg_ones) * (1.0 / HEAD_DIM)
    return x * lax.rsqrt(ms + EPS) * g


def _sigmoid(x):
    return 1.0 / (1.0 + jnp.exp(-x))


def _softplus(z):
    return jnp.maximum(z, 0.0) + jnp.log1p(jnp.exp(-jnp.abs(z)))


def _proj_in_kernel(x_ref, g_ref, w_ref, o_ref):
    h = _rms_rows(x_ref[...], g_ref[...]).astype(BF16)
    o_ref[...] = jnp.dot(h, w_ref[...], preferred_element_type=F32)


def proj_in(x, g, w):
    n, d = x.shape
    m = w.shape[1]
    tm = min(ROW_TILE, n)
    return pl.pallas_call(
        _proj_in_kernel,
        grid=(n // tm,),
        in_specs=[pl.BlockSpec((tm, d), lambda i: (i, 0)),
                  pl.BlockSpec((1, d), lambda i: (0, 0)),
                  pl.BlockSpec((d, m), lambda i: (0, 0))],
        out_specs=pl.BlockSpec((tm, m), lambda i: (i, 0)),
        out_shape=jax.ShapeDtypeStruct((n, m), F32),
        compiler_params=_params("parallel"),
        name="proj_in",
    )(x, g.reshape(1, d), w)


def _out_ffn_kernel(o_ref, om_ref, x_ref, w1_ref, w2_ref, gf_ref, wg_ref, wu_ref, wo_ref, y_ref):
    x = x_ref[...] + _dot(o_ref[...], w1_ref[...]) + _dot(om_ref[...], w2_ref[...])
    h = _rms_rows(x, gf_ref[...]).astype(BF16)

    def chunk(c, acc):
        gate = jnp.dot(h, wg_ref[c], preferred_element_type=F32)
        up = jnp.dot(h, wu_ref[c], preferred_element_type=F32)
        act = (gate * _sigmoid(gate) * up).astype(BF16)
        return acc + jnp.dot(act, wo_ref[c], preferred_element_type=F32)

    y_ref[...] = lax.fori_loop(0, wg_ref.shape[0], chunk, x)


def out_ffn(o, om, x, w1, w2, gf, wg, wu, wo):
    n, d = x.shape
    tm = min(ROW_TILE, n)
    const = lambda *shape: pl.BlockSpec(shape, lambda i: (0,) * len(shape))
    return pl.pallas_call(
        _out_ffn_kernel,
        grid=(n // tm,),
        in_specs=[pl.BlockSpec((tm, o.shape[1]), lambda i: (i, 0)),
                  pl.BlockSpec((tm, om.shape[1]), lambda i: (i, 0)),
                  pl.BlockSpec((tm, d), lambda i: (i, 0)),
                  const(*w1.shape), const(*w2.shape), const(1, d),
                  const(*wg.shape), const(*wu.shape), const(*wo.shape)],
        out_specs=pl.BlockSpec((tm, d), lambda i: (i, 0)),
        out_shape=jax.ShapeDtypeStruct((n, d), F32),
        compiler_params=_params("parallel"),
        name="out_ffn",
    )(o, om, x, w1, w2, gf.reshape(1, d), wg, wu, wo)


def _mem_kv_kernel(mem_ref, gn_ref, w_ref, gk_ref, seg_ref, o_ref):
    h = _rms_rows(mem_ref[...], gn_ref[0]).astype(BF16)
    kv = jnp.dot(h, w_ref[0], preferred_element_type=F32)
    seg = seg_ref[...]
    for c in range(MEM_WIDTH // LANES):
        sl = slice(c * LANES, (c + 1) * LANES)
        o_ref[0, :, sl] = _head_rms(kv[:, sl], seg, gk_ref[0])
    o_ref[0, :, MEM_WIDTH:] = kv[:, MEM_WIDTH:]


def mem_kv_all(mem, g_norm, w_kv, g_k, seg):
    n, d = mem.shape
    nl = w_kv.shape[0]
    return pl.pallas_call(
        _mem_kv_kernel,
        grid=(nl,),
        in_specs=[pl.BlockSpec((n, d), lambda l: (0, 0)),
                  pl.BlockSpec((1, 1, d), lambda l: (l, 0, 0)),
                  pl.BlockSpec((1, d, 2 * MEM_WIDTH), lambda l: (l, 0, 0)),
                  pl.BlockSpec((1, 1, LANES), lambda l: (l, 0, 0)),
                  pl.BlockSpec((LANES, LANES), lambda l: (0, 0))],
        out_specs=pl.BlockSpec((1, n, 2 * MEM_WIDTH), lambda l: (l, 0, 0)),
        out_shape=jax.ShapeDtypeStruct((nl, n, 2 * MEM_WIDTH), F32),
        compiler_params=_params("arbitrary"),
        name="mem_kv",
    )(mem, g_norm.reshape(nl, 1, d), w_kv, g_k.reshape(nl, 1, LANES), seg)


def _mem_query(qm, seg, gq):
    parts = [_head_rms(qm[:, c * LANES:(c + 1) * LANES], seg, gq) for c in range(MEM_WIDTH // LANES)]
    return jnp.concatenate(parts, axis=1) * SCORE_SCALE


def _mem_attend(qn, kv):
    r = qn.shape[0]
    head_of_lane = _iota((1, MEM_WIDTH), 1) >> 6
    qz = jnp.concatenate([jnp.where(head_of_lane == h, qn, 0.0) for h in range(MEM_HEADS)], axis=0)
    s = _dot_nt(qz, kv[:, :MEM_WIDTH])
    p = jnp.exp(s - jnp.max(s, axis=-1, keepdims=True))
    p = p / jnp.sum(p, axis=-1, keepdims=True)
    res = _dot(p, kv[:, MEM_WIDTH:])
    out = jnp.zeros((r, MEM_WIDTH), F32)
    for h in range(MEM_HEADS):
        out = out + jnp.where(head_of_lane == h, res[h * r:(h + 1) * r], 0.0)
    return out


def _mem_attn_kernel(qm_ref, kv_ref, gq_ref, seg_ref, o_ref):
    qn = _mem_query(qm_ref[...], seg_ref[...], gq_ref[...])
    o_ref[...] = _mem_attend(qn, kv_ref[...])


def mem_attn_prompt(u, mem_col, kv, gq, seg, n_batch):
    n = u.shape[0]
    t = n // n_batch
    tm = 256
    n_mem = kv.shape[0] // n_batch
    nt = t // tm
    return pl.pallas_call(
        _mem_attn_kernel,
        grid=(n_batch, nt),
        in_specs=[pl.BlockSpec((tm, MEM_WIDTH), lambda b, i: (b * nt + i, mem_col)),
                  pl.BlockSpec((n_mem, 2 * MEM_WIDTH), lambda b, i: (b, 0)),
                  pl.BlockSpec((1, LANES), lambda b, i: (0, 0)),
                  pl.BlockSpec((LANES, LANES), lambda b, i: (0, 0))],
        out_specs=pl.BlockSpec((tm, MEM_WIDTH), lambda b, i: (b * nt + i, 0)),
        out_shape=jax.ShapeDtypeStruct((n, MEM_WIDTH), F32),
        compiler_params=_params("parallel", "parallel"),
        name="mem_attn_prompt",
    )(u, kv, gq, seg)


def _mem_attn_sample_kernel(qm_ref, kv_ref, gq_ref, seg_ref, o_ref):
    qm = jnp.broadcast_to(qm_ref[0], (SUBLANES, MEM_WIDTH))
    qn = _mem_query(qm, seg_ref[...], gq_ref[...])
    o_ref[0] = _mem_attend(qn, kv_ref[0])[0:1]


def mem_attn_sample(qm, kv, gq, seg):
    s, n_mem, _ = kv.shape
    return pl.pallas_call(
        _mem_attn_sample_kernel,
        grid=(s,),
        in_specs=[pl.BlockSpec((1, 1, MEM_WIDTH), lambda b: (b, 0, 0)),
                  pl.BlockSpec((1, n_mem, 2 * MEM_WIDTH), lambda b: (b, 0, 0)),
                  pl.BlockSpec((1, LANES), lambda b: (0, 0)),
                  pl.BlockSpec((LANES, LANES), lambda b: (0, 0))],
        out_specs=pl.BlockSpec((1, 1, MEM_WIDTH), lambda b: (b, 0, 0)),
        out_shape=jax.ShapeDtypeStruct((s, 1, MEM_WIDTH), F32),
        compiler_params=_params("parallel"),
        name="mem_attn_sample",
    )(qm, kv, gq, seg)


def _nsa_prep_kernel(q_ref, kv_ref, gt_ref, cos_ref, sa_ref, sb_ref, gq_ref, gkv_ref, seg_ref,
                     qo_ref, cso_ref, wino_ref, go_ref):
    cos, sa, sb = cos_ref[...], sa_ref[...], sb_ref[...]
    seg = seg_ref[...]
    half = ROT_DIM // 2

    def norm_rope(x, g):
        y = _head_rms(x, seg, g)
        return y * cos + pltpu.roll(y, half, 1) * sa + pltpu.roll(y, LANES - half, 1) * sb

    for c in range(MIXER_WIDTH // LANES):
        sl = slice(c * LANES, (c + 1) * LANES)
        qo_ref[:, sl] = norm_rope(q_ref[:, sl], gq_ref[...])
    for c in range(6):
        x = kv_ref[:, c * LANES:(c + 1) * LANES]
        if c % 2 == 0:
            x = norm_rope(x, gkv_ref[c // 2])
        if c < 4:
            cso_ref[:, c * LANES:(c + 1) * LANES] = x
        else:
            wino_ref[:, (c - 4) * LANES:(c - 3) * LANES] = x
    go_ref[...] = _sigmoid(gt_ref[...])


def nsa_prep(u, rope_tabs, gq, gkv, seg, n_time_tiles, tm):
    n = u.shape[0]
    cos, sa, sb = rope_tabs
    tab = pl.BlockSpec((tm, LANES), lambda i: (i % n_time_tiles, 0))
    const = lambda *shape: pl.BlockSpec(shape, lambda i: (0,) * len(shape))
    return pl.pallas_call(
        _nsa_prep_kernel,
        grid=(n // tm,),
        in_specs=[pl.BlockSpec((tm, MIXER_WIDTH), lambda i: (i, 0)),
                  pl.BlockSpec((tm, MIXER_WIDTH), lambda i: (i, 1)),
                  pl.BlockSpec((tm, LANES), lambda i: (i, (2 * MIXER_WIDTH + MEM_WIDTH) // LANES)),
                  tab, tab, tab, const(1, LANES), const(3, 1, LANES), const(LANES, LANES)],
        out_specs=[pl.BlockSpec((tm, MIXER_WIDTH), lambda i: (i, 0)),
                   pl.BlockSpec((tm, 4 * LANES), lambda i: (i, 0)),
                   pl.BlockSpec((tm, 2 * LANES), lambda i: (i, 0)),
                   pl.BlockSpec((tm, LANES), lambda i: (i, 0))],
        out_shape=[jax.ShapeDtypeStruct((n, MIXER_WIDTH), F32),
                   jax.ShapeDtypeStruct((n, 4 * LANES), F32),
                   jax.ShapeDtypeStruct((n, 2 * LANES), F32),
                   jax.ShapeDtypeStruct((n, LANES), F32)],
        compiler_params=_params("parallel"),
        name="nsa_prep",
    )(u, u, u, cos, sa, sb, gq, gkv, seg)


def _compress_kernel(xk_ref, xv_ref, w_ref, o_ref):
    mb = o_ref.shape[0]
    for c, x_ref in enumerate((xk_ref, xv_ref)):
        acc = jnp.zeros((mb, LANES), F32)
        for r in range(NSA_BLOCK):
            xr = x_ref[pl.ds(r, mb, stride=NSA_BLOCK), :]
            acc = acc + jnp.dot(xr.astype(BF16), w_ref[c, r], preferred_element_type=F32)
        o_ref[:, c * LANES:(c + 1) * LANES] = acc


def _pick_tile(n, cap):
    return max(t for t in range(SUBLANES, cap + 1, SUBLANES) if n % t == 0)


def compress(cs, wbd):
    nblk = cs.shape[0] // NSA_BLOCK
    mb = _pick_tile(nblk, CMP_BLOCKS)
    return pl.pallas_call(
        _compress_kernel,
        grid=(nblk // mb,),
        in_specs=[pl.BlockSpec((mb * NSA_BLOCK, LANES), lambda i: (i, 0)),
                  pl.BlockSpec((mb * NSA_BLOCK, LANES), lambda i: (i, 1)),
                  pl.BlockSpec(wbd.shape, lambda i: (0, 0, 0, 0))],
        out_specs=pl.BlockSpec((mb, 2 * LANES), lambda i: (i, 0)),
        out_shape=jax.ShapeDtypeStruct((nblk, 2 * LANES), F32),
        compiler_params=_params("parallel"),
        name="nsa_compress",
    )(cs, cs, wbd)


def _group_queries(q_chunks, g):
    parts = []
    for h in range(NSA_HPG):
        hh = g * NSA_HPG + h
        chunk = q_chunks(hh // 2)
        if hh % 2 != g:
            chunk = pltpu.roll(chunk, HEAD_DIM, 1)
        parts.append(chunk)
    qz = jnp.concatenate(parts, axis=0) * SCORE_SCALE
    return jnp.where((_iota((1, LANES), 1) >> 6) == g, qz, 0.0)


def _select_blocks(imp, cur):
    r, nb = imp.shape
    nidx = _iota((1, nb), 1)
    score = jnp.where(nidx == cur, BIG, jnp.where(nidx < cur, imp, -1.0))
    rank = jnp.zeros((r, nb), F32)
    for m in range(nb):
        col = score[:, m:m + 1]
        rank = rank + jnp.where(nidx > m, jnp.where(col >= score, 1.0, 0.0), jnp.where(col > score, 1.0, 0.0))
    return (rank < float(NSA_TOPK)) & (nidx <= cur)


def _masked_softmax_parts(s, ok):
    m = jnp.max(jnp.where(ok, s, -BIG), axis=-1, keepdims=True)
    p = jnp.where(ok, jnp.exp(s - m), 0.0)
    return p, jnp.maximum(jnp.sum(p, axis=-1, keepdims=True), 1e-30)


def _place_heads(res, g, rows):
    low = _iota((1, LANES), 1) < HEAD_DIM
    out = []
    for c in range(NSA_HPG // 2):
        a = res[(2 * c) * rows:(2 * c + 1) * rows]
        b = res[(2 * c + 1) * rows:(2 * c + 2) * rows]
        if g == 1:
            a = pltpu.roll(a, HEAD_DIM, 1)
        else:
            b = pltpu.roll(b, HEAD_DIM, 1)
        out.append(jnp.where(low, a, b))
    return out


def _nsa_attn_kernel(q_ref, g_ref, cs_ref, win_ref, cmp_ref, gexp_ref, o_ref):
    qt = pl.program_id(1)
    t0 = qt * Q_TILE
    seq = cs_ref.shape[0]
    nb = cmp_ref.shape[0]
    tpos = t0 + _iota((Q_TILE, 1), 0)
    cur = tpos >> 6
    rows = NSA_HPG * Q_TILE

    gates = g_ref[...]
    gexp = [_dot_x2(gates, gexp_ref[j]) for j in range(3)]

    win_span = min(NSA_WINDOW + Q_TILE, seq)
    w0 = pl.multiple_of(jnp.maximum(t0 + Q_TILE - win_span, 0), Q_TILE)
    n_chunks = (t0 + Q_TILE + SEL_CHUNK - 1) // SEL_CHUNK

    branch_out = []
    for g in range(NSA_GROUPS):
        qz = _group_queries(lambda c: q_ref[:, c * LANES:(c + 1) * LANES], g)

        s = _dot_nt_precise(qz, cmp_ref[:, 0:LANES]).reshape(NSA_HPG, Q_TILE, nb)
        nidx = _iota((1, nb), 1)
        complete = ((nidx + 1) * NSA_BLOCK <= tpos + 1)[None]
        p, den = _masked_softmax_parts(s, complete)
        p_cmp = p / den
        o_cmp = _dot(p_cmp.reshape(rows, nb), cmp_ref[:, LANES:2 * LANES])
        sel = _select_blocks(jnp.sum(p_cmp, axis=0), cur)
        sel_bf = jnp.where(sel, 1.0, 0.0).astype(BF16)

        def sel_chunk(kc, carry):
            m, l, acc = carry
            k0 = pl.multiple_of(kc * SEL_CHUNK, SEL_CHUNK)
            kch = cs_ref[pl.ds(k0, SEL_CHUNK), 2 * LANES:3 * LANES]
            vch = cs_ref[pl.ds(k0, SEL_CHUNK), 3 * LANES:4 * LANES]
            sc = _dot_nt(qz, kch).reshape(NSA_HPG, Q_TILE, SEL_CHUNK)
            kpos = k0 + _iota((1, SEL_CHUNK), 1)
            expand = jnp.where((kpos >> 6) == _iota((nb, 1), 0), 1.0, 0.0).astype(BF16)
            chosen = jnp.dot(sel_bf, expand, preferred_element_type=F32)
            ok = ((chosen > 0.5) & (kpos <= tpos))[None]
            m_new = jnp.maximum(m, jnp.max(jnp.where(ok, sc, -BIG), axis=-1, keepdims=True))
            alpha = jnp.exp(m - m_new)
            pp = jnp.where(ok, jnp.exp(sc - m_new), 0.0)
            l = alpha * l + jnp.sum(pp, axis=-1, keepdims=True)
            pv = _dot(pp.reshape(rows, SEL_CHUNK), vch).reshape(NSA_HPG, Q_TILE, LANES)
            return m_new, l, alpha * acc + pv

        init = (jnp.full((NSA_HPG, Q_TILE, 1), -BIG, F32), jnp.zeros((NSA_HPG, Q_TILE, 1), F32),
                jnp.zeros((NSA_HPG, Q_TILE, LANES), F32))
        _, l, acc = lax.fori_loop(0, n_chunks, sel_chunk, init)
        o_slc = (acc / jnp.maximum(l, 1e-30)).reshape(rows, LANES)

        kw = win_ref[pl.ds(w0, win_span), 0:LANES]
        vw = win_ref[pl.ds(w0, win_span), LANES:2 * LANES]
        sw = _dot_nt(qz, kw).reshape(NSA_HPG, Q_TILE, win_span)
        wpos = w0 + _iota((1, win_span), 1)
        w_ok = ((wpos <= tpos) & (wpos > tpos - NSA_WINDOW))[None]
        pw, denw = _masked_softmax_parts(sw, w_ok)
        o_win = _dot((pw / denw).reshape(rows, win_span), vw)

        branch_out.append([_place_heads(o, g, Q_TILE) for o in (o_cmp, o_slc, o_win)])

    for c in range(MIXER_WIDTH // LANES):
        g, cc = divmod(c, NSA_HPG // 2)
        sl = slice(c * LANES, (c + 1) * LANES)
        o_ref[:, sl] = sum(gexp[j][:, sl] * branch_out[g][j][cc] for j in range(3))


def nsa_attn_prompt(q, gates, cs, win, cmp, gexp, n_batch):
    n = q.shape[0]
    t = n // n_batch
    nt = t // Q_TILE
    nb = cmp.shape[0] // n_batch
    return pl.pallas_call(
        _nsa_attn_kernel,
        grid=(n_batch, nt),
        in_specs=[pl.BlockSpec((Q_TILE, MIXER_WIDTH), lambda b, i: (b * nt + i, 0)),
                  pl.BlockSpec((Q_TILE, LANES), lambda b, i: (b * nt + i, 0)),
                  pl.BlockSpec((t, 4 * LANES), lambda b, i: (b, 0)),
                  pl.BlockSpec((t, 2 * LANES), lambda b, i: (b, 0)),
                  pl.BlockSpec((nb, 2 * LANES), lambda b, i: (b, 0)),
                  pl.BlockSpec(gexp.shape, lambda b, i: (0, 0, 0))],
        out_specs=pl.BlockSpec((Q_TILE, MIXER_WIDTH), lambda b, i: (b * nt + i, 0)),
        out_shape=jax.ShapeDtypeStruct((n, MIXER_WIDTH), F32),
        compiler_params=_params("parallel", "parallel"),
        name="nsa_attn_prompt",
    )(q, gates, cs, win, cmp, gexp)


SAMPLE_ROWS = 16
NB_PAD = 64


def _sample_queries(q_row):
    row = _iota((SAMPLE_ROWS, 1), 0)
    lane_group = _iota((1, LANES), 1) >> 6
    qz = jnp.zeros((SAMPLE_ROWS, LANES), F32)
    for hh in range(NSA_HEADS):
        g = hh // NSA_HPG
        chunk = jnp.broadcast_to(q_row[:, (hh // 2) * LANES:(hh // 2 + 1) * LANES], (SAMPLE_ROWS, LANES))
        if hh % 2 != g:
            chunk = pltpu.roll(chunk, HEAD_DIM, 1)
        qz = qz + jnp.where((row == hh) & (lane_group == g), chunk, 0.0)
    return qz * SCORE_SCALE


def _nsa_sample_cmp_kernel(pt_ref, q_ref, pool_ref, qz_ref, ocmp_ref, mask_ref, cm_ref, *, past):
    b = pl.program_id(0)
    n_pages = past // PAGE_SIZE
    cm_ref[...] = jnp.zeros(cm_ref.shape, F32)
    for p in range(n_pages):
        row = pool_ref[pl.ds(pt_ref[b * n_pages + p], 1), :]
        cm_ref[2 * p:2 * p + 1, :] = row[:, 0:2 * LANES]
        cm_ref[2 * p + 1:2 * p + 2, :] = row[:, 2 * LANES:4 * LANES]
    qz = _sample_queries(q_ref[0])
    qz_ref[0] = qz

    s = _dot_nt_precise(qz, cm_ref[:, 0:LANES])
    nidx = _iota((1, NB_PAD), 1)
    complete = (nidx + 1) * NSA_BLOCK <= past + 1
    p, den = _masked_softmax_parts(s, complete)
    p_cmp = p / den
    ocmp_ref[0] = _dot(p_cmp, cm_ref[:, LANES:2 * LANES])

    head = _iota((SAMPLE_ROWS, 1), 0)
    row8 = _iota((SUBLANES, 1), 0)
    imp = jnp.zeros((SUBLANES, NB_PAD), F32)
    for g in range(NSA_GROUPS):
        in_group = (head >= g * NSA_HPG) & (head < (g + 1) * NSA_HPG)
        imp = imp + jnp.where(row8 == g, jnp.sum(jnp.where(in_group, p_cmp, 0.0), axis=0, keepdims=True), 0.0)
    cur = jnp.full((SUBLANES, 1), past // NSA_BLOCK, jnp.int32)
    sel = jnp.where(_select_blocks(imp, cur), 1.0, 0.0)
    sel_rows = jnp.where(head < NSA_HPG, sel[0:1], sel[1:2]).astype(BF16)
    kpos = _iota((1, past), 1)
    expand = jnp.where((kpos >> 6) == _iota((NB_PAD, 1), 0), 1.0, 0.0).astype(BF16)
    mask_ref[0] = jnp.dot(sel_rows, expand, preferred_element_type=F32)


def nsa_sample_cmp(page_table, q, cmp_pool, past):
    s = q.shape[0]
    grid_spec = pltpu.PrefetchScalarGridSpec(
        num_scalar_prefetch=1,
        grid=(s,),
        in_specs=[pl.BlockSpec((1, 1, MIXER_WIDTH), lambda b, pt: (b, 0, 0)),
                  pl.BlockSpec(cmp_pool.shape, lambda b, pt: (0, 0))],
        out_specs=[pl.BlockSpec((1, SAMPLE_ROWS, LANES), lambda b, pt: (b, 0, 0)),
                   pl.BlockSpec((1, SAMPLE_ROWS, LANES), lambda b, pt: (b, 0, 0)),
                   pl.BlockSpec((1, SAMPLE_ROWS, past), lambda b, pt: (b, 0, 0))],
        scratch_shapes=[pltpu.VMEM((NB_PAD, 2 * LANES), F32)],
    )
    return pl.pallas_call(
        functools.partial(_nsa_sample_cmp_kernel, past=past),
        grid_spec=grid_spec,
        out_shape=[jax.ShapeDtypeStruct((s, SAMPLE_ROWS, LANES), F32),
                   jax.ShapeDtypeStruct((s, SAMPLE_ROWS, LANES), F32),
                   jax.ShapeDtypeStruct((s, SAMPLE_ROWS, past), F32)],
        compiler_params=_params("arbitrary"),
        name="nsa_sample_cmp",
    )(page_table.reshape(-1), q, cmp_pool)


def _nsa_sample_sel_kernel(pt_ref, qz_ref, mask_ref, page_ref, new_ref, o_ref, m_ref, l_ref, acc_ref):
    p = pl.program_id(1)

    @pl.when(p == 0)
    def _():
        m_ref[...] = jnp.full(m_ref.shape, -BIG, F32)
        l_ref[...] = jnp.zeros(l_ref.shape, F32)
        acc_ref[...] = jnp.zeros(acc_ref.shape, F32)

    qz = qz_ref[0]
    s = _dot_nt(qz, page_ref[0, :, 0:LANES])
    ok = mask_ref[0] > 0.5
    m_new = jnp.maximum(m_ref[...], jnp.max(jnp.where(ok, s, -BIG), axis=-1, keepdims=True))
    alpha = jnp.exp(m_ref[...] - m_new)
    pp = jnp.where(ok, jnp.exp(s - m_new), 0.0)
    l_ref[...] = alpha * l_ref[...] + jnp.sum(pp, axis=-1, keepdims=True)
    acc_ref[...] = alpha * acc_ref[...] + _dot(pp, page_ref[0, :, LANES:2 * LANES])
    m_ref[...] = m_new

    @pl.when(p == pl.num_programs(1) - 1)
    def _():
        k_new = new_ref[0, :, 2 * LANES:3 * LANES]
        v_new = new_ref[0, :, 3 * LANES:4 * LANES]
        s_new = jnp.sum(qz.astype(BF16).astype(F32) * k_new.astype(BF16).astype(F32), axis=-1, keepdims=True)
        m_fin = jnp.maximum(m_ref[...], s_new)
        a = jnp.exp(m_ref[...] - m_fin)
        p_new = jnp.exp(s_new - m_fin)
        den = a * l_ref[...] + p_new
        o_ref[0] = (a * acc_ref[...] + p_new.astype(BF16).astype(F32) * v_new.astype(BF16).astype(F32)) / den


def nsa_sample_sel(page_table, qz, mask, pool, cs_new):
    s, n_pages = page_table.shape
    grid_spec = pltpu.PrefetchScalarGridSpec(
        num_scalar_prefetch=1,
        grid=(s, n_pages),
        in_specs=[pl.BlockSpec((1, SAMPLE_ROWS, LANES), lambda b, p, pt: (b, 0, 0)),
                  pl.BlockSpec((1, SAMPLE_ROWS, PAGE_SIZE), lambda b, p, pt: (b, 0, p)),
                  pl.BlockSpec((1, PAGE_SIZE, 2 * LANES), lambda b, p, pt: (pt[b * n_pages + p], 0, 1)),
                  pl.BlockSpec((1, 1, 4 * LANES), lambda b, p, pt: (b, 0, 0))],
        out_specs=pl.BlockSpec((1, SAMPLE_ROWS, LANES), lambda b, p, pt: (b, 0, 0)),
        scratch_shapes=[pltpu.VMEM((SAMPLE_ROWS, 1), F32), pltpu.VMEM((SAMPLE_ROWS, 1), F32),
                        pltpu.VMEM((SAMPLE_ROWS, LANES), F32)],
    )
    return pl.pallas_call(
        _nsa_sample_sel_kernel,
        grid_spec=grid_spec,
        out_shape=jax.ShapeDtypeStruct((s, SAMPLE_ROWS, LANES), F32),
        compiler_params=_params("parallel", "arbitrary"),
        name="nsa_sample_sel",
    )(page_table.reshape(-1), qz, mask, pool, cs_new)


def _nsa_sample_win_kernel(qz_ref, win_ref, new_ref, ocmp_ref, oslc_ref, g_ref, gexp_ref, o_ref):
    qz = qz_ref[0]
    lw = win_ref.shape[1]
    s = _dot_nt(qz, win_ref[0, :, 0:LANES])
    ok = _iota((1, lw), 1) > lw - NSA_WINDOW
    k_new = new_ref[0, :, 0:LANES]
    v_new = new_ref[0, :, LANES:2 * LANES]
    s_new = jnp.sum(qz.astype(BF16).astype(F32) * k_new.astype(BF16).astype(F32), axis=-1, keepdims=True)
    m = jnp.maximum(jnp.max(jnp.where(ok, s, -BIG), axis=-1, keepdims=True), s_new)
    pp = jnp.where(ok, jnp.exp(s - m), 0.0)
    p_new = jnp.exp(s_new - m)
    den = jnp.sum(pp, axis=-1, keepdims=True) + p_new
    o_win = (_dot(pp / den, win_ref[0, :, LANES:2 * LANES])
             + (p_new / den).astype(BF16).astype(F32) * v_new.astype(BF16).astype(F32))

    gates = jnp.broadcast_to(g_ref[0], (SUBLANES, LANES))
    head = _iota((SAMPLE_ROWS, 1), 0)
    lane_head = _iota((1, MIXER_WIDTH), 1) >> 6
    in_place = (head & 1) == jnp.where(head >= NSA_HPG, 1, 0)
    out = jnp.zeros((1, MIXER_WIDTH), F32)
    for j, o in enumerate((ocmp_ref[0], oslc_ref[0], o_win)):
        tiled = jnp.concatenate([o] * (MIXER_WIDTH // LANES), axis=1)
        rolled = jnp.concatenate([pltpu.roll(o, HEAD_DIM, 1)] * (MIXER_WIDTH // LANES), axis=1)
        placed = jnp.where(lane_head == head, jnp.where(in_place, tiled, rolled), 0.0)
        flat = jnp.sum(placed, axis=0, keepdims=True)
        out = out + _dot_x2(gates, gexp_ref[j])[0:1] * flat
    o_ref[0] = out


def nsa_sample_win(qz, win_state, win_new, ocmp, oslc, gates, gexp):
    s, lw, _ = win_state.shape
    per_seq = lambda *shape: pl.BlockSpec((1,) + shape, lambda b: (b,) + (0,) * len(shape))
    return pl.pallas_call(
        _nsa_sample_win_kernel,
        grid=(s,),
        in_specs=[per_seq(SAMPLE_ROWS, LANES), per_seq(lw, 2 * LANES), per_seq(1, 2 * LANES),
                  per_seq(SAMPLE_ROWS, LANES), per_seq(SAMPLE_ROWS, LANES), per_seq(1, LANES),
                  pl.BlockSpec(gexp.shape, lambda b: (0, 0, 0))],
        out_specs=per_seq(1, MIXER_WIDTH),
        out_shape=jax.ShapeDtypeStruct((s, 1, MIXER_WIDTH), F32),
        compiler_params=_params("parallel"),
        name="nsa_sample_win",
    )(qz, win_state, win_new, ocmp, oslc, gates, gexp)


def _layer_norm_swish(y, g, b):
    mu = jnp.mean(y, axis=-1, keepdims=True)
    var = jnp.mean(jnp.square(y - mu), axis=-1, keepdims=True)
    z = (y - mu) * lax.rsqrt(var + EPS) * g + b
    return z * _sigmoid(z)


def _conv_prompt_kernel(a_ref, b_ref, ap_ref, bp_ref, w_ref, cb_ref, lg_ref, lb_ref, o_ref, st_ref, buf_ref):
    i = pl.program_id(1)
    glu = a_ref[...] * _sigmoid(b_ref[...])
    prev = ap_ref[...] * _sigmoid(bp_ref[...])
    buf_ref[0:CONV_HALO, :] = jnp.where(i > 0, prev, 0.0)
    buf_ref[CONV_HALO:, :] = glu
    tt = a_ref.shape[0]
    acc = jnp.zeros(glu.shape, F32)
    for k in range(CONV_WIDTH):
        acc = acc + w_ref[k:k + 1, :] * buf_ref[pl.ds(CONV_HALO - (CONV_WIDTH - 1) + k, tt), :]
    o_ref[...] = _layer_norm_swish(acc + cb_ref[...], lg_ref[...], lb_ref[...])

    @pl.when(i == pl.num_programs(1) - 1)
    def _():
        st_ref[0] = glu[tt - CONV_HALO:, :]


def conv_prompt(u, w, cb, lg, lb, n_batch):
    n = u.shape[0]
    t = n // n_batch
    tt = min(CONV_TILE, t)
    nt = t // tt
    ch = MIXER_WIDTH
    ratio = tt // CONV_HALO
    cur = lambda col: pl.BlockSpec((tt, ch), lambda b, i: (b * nt + i, col))
    prev = lambda col: pl.BlockSpec((CONV_HALO, ch), lambda b, i: (jnp.maximum((b * nt + i) * ratio - 1, 0), col))
    const = lambda *shape: pl.BlockSpec(shape, lambda b, i: (0,) * len(shape))
    return pl.pallas_call(
        _conv_prompt_kernel,
        grid=(n_batch, nt),
        in_specs=[cur(0), cur(1), prev(0), prev(1), const(CONV_HALO, ch), const(1, ch), const(1, ch), const(1, ch)],
        out_specs=[pl.BlockSpec((tt, ch), lambda b, i: (b * nt + i, 0)),
                   pl.BlockSpec((1, CONV_HALO, ch), lambda b, i: (b, 0, 0))],
        out_shape=[jax.ShapeDtypeStruct((n, ch), F32),
                   jax.ShapeDtypeStruct((n_batch, CONV_HALO, ch), F32)],
        scratch_shapes=[pltpu.VMEM((CONV_HALO + tt, ch), F32)],
        compiler_params=_params("parallel", "arbitrary"),
        name="conv_prompt",
    )(u, u, u, u, w, cb, lg, lb)


def _conv_sample_kernel(a_ref, b_ref, st_ref, w_ref, cb_ref, lg_ref, lb_ref, o_ref, glu_ref):
    glu = a_ref[...] * _sigmoid(b_ref[...])
    acc = w_ref[CONV_WIDTH - 1:CONV_WIDTH, :] * glu
    for k in range(CONV_WIDTH - 1):
        acc = acc + w_ref[k:k + 1, :] * st_ref[k]
    o_ref[...] = _layer_norm_swish(acc + cb_ref[...], lg_ref[...], lb_ref[...])
    glu_ref[...] = glu


def conv_sample(u, state_t, w, cb, lg, lb):
    s = u.shape[0]
    ch = MIXER_WIDTH
    const = lambda *shape: pl.BlockSpec(shape, lambda i: (0,) * len(shape))
    return pl.pallas_call(
        _conv_sample_kernel,
        grid=(1,),
        in_specs=[pl.BlockSpec((s, ch), lambda i: (0, 0)), pl.BlockSpec((s, ch), lambda i: (0, 1)),
                  const(CONV_WIDTH - 1, s, ch), const(CONV_HALO, ch), const(1, ch), const(1, ch), const(1, ch)],
        out_specs=[const(s, ch), const(s, ch)],
        out_shape=[jax.ShapeDtypeStruct((s, ch), F32), jax.ShapeDtypeStruct((s, ch), F32)],
        compiler_params=_params("arbitrary"),
        name="conv_sample",
    )(u, u, state_t, w, cb, lg, lb)


def _stick_weights(z, valid, later, tri):
    sp = _softplus(z)
    log_stay = -sp if valid is None else jnp.where(valid, -sp, 0.0)
    within = _dot_x3(log_stay, tri)
    a = jnp.exp(z - sp + within + later)
    if valid is not None:
        a = jnp.where(valid, a, 0.0)
    return a, within[:, 0:1] + log_stay[:, 0:1]


def _sb_attn_kernel(q_ref, k_ref, v_ref, tri_ref, o_ref):
    qt = pl.program_id(2)
    t0 = qt * Q_TILE
    low = _iota((1, LANES), 1) < HEAD_DIM
    q2 = q_ref[...] * SCORE_SCALE
    qz = jnp.concatenate([jnp.where(low, q2, 0.0), jnp.where(low, 0.0, q2)], axis=0)
    tpos = t0 + (_iota((2 * Q_TILE, 1), 0) & (Q_TILE - 1))
    tri = tri_ref[...]

    def chunk(i, carry):
        later, acc = carry
        k0 = pl.multiple_of((qt - i) * SB_CHUNK, SB_CHUNK)
        z = _dot_nt(qz, k_ref[pl.ds(k0, SB_CHUNK), :])
        valid = (k0 + _iota((1, SB_CHUNK), 1)) < tpos
        a, total = _stick_weights(z, valid, later, tri)
        return later + total, acc + _dot(a, v_ref[pl.ds(k0, SB_CHUNK), :])

    init = (jnp.zeros((2 * Q_TILE, 1), F32), jnp.zeros((2 * Q_TILE, LANES), F32))
    _, acc = lax.fori_loop(0, qt + 1, chunk, init)
    o_ref[...] = jnp.where(low, acc[:Q_TILE], acc[Q_TILE:])


def sb_attn_prompt(u, tri, n_batch):
    n = u.shape[0]
    t = n // n_batch
    nt = t // Q_TILE
    pairs = MIXER_WIDTH // LANES
    return pl.pallas_call(
        _sb_attn_kernel,
        grid=(n_batch, pairs, nt),
        in_specs=[pl.BlockSpec((Q_TILE, LANES), lambda b, j, i: (b * nt + i, j)),
                  pl.BlockSpec((t, LANES), lambda b, j, i: (b, pairs + j)),
                  pl.BlockSpec((t, LANES), lambda b, j, i: (b, 2 * pairs + j)),
                  pl.BlockSpec(tri.shape, lambda b, j, i: (0, 0))],
        out_specs=pl.BlockSpec((Q_TILE, LANES), lambda b, j, i: (b * nt + i, j)),
        out_shape=jax.ShapeDtypeStruct((n, MIXER_WIDTH), F32),
        compiler_params=_params("parallel", "parallel", "parallel"),
        name="sb_attn_prompt",
    )(u, u, u, tri)


def _sb_sample_kernel(pt_ref, q_ref, page_ref, tri_ref, o_ref, later_ref, acc_ref):
    p = pl.program_id(1)

    @pl.when(p == 0)
    def _():
        later_ref[...] = jnp.zeros(later_ref.shape, F32)
        acc_ref[...] = jnp.zeros(acc_ref.shape, F32)

    head = _iota((SAMPLE_ROWS, 1), 0)
    lane_head = _iota((1, MIXER_WIDTH), 1) >> 6
    qz = jnp.where(lane_head == head, jnp.broadcast_to(q_ref[0], (SAMPLE_ROWS, MIXER_WIDTH)), 0.0) * SCORE_SCALE
    z = _dot_nt(qz, page_ref[0, :, 0:MIXER_WIDTH])
    a, total = _stick_weights(z, None, later_ref[...], tri_ref[...])
    acc_ref[...] += _dot(a, page_ref[0, :, MIXER_WIDTH:2 * MIXER_WIDTH])
    later_ref[...] += total

    @pl.when(p == pl.num_programs(1) - 1)
    def _():
        o_ref[0] = jnp.sum(jnp.where(lane_head == head, acc_ref[...], 0.0), axis=0, keepdims=True)


def sb_attn_sample(page_table, q, pool, tri):
    s, n_pages = page_table.shape
    grid_spec = pltpu.PrefetchScalarGridSpec(
        num_scalar_prefetch=1,
        grid=(s, n_pages),
        in_specs=[pl.BlockSpec((1, 1, MIXER_WIDTH), lambda b, p, pt: (b, 0, 0)),
                  pl.BlockSpec((1, PAGE_SIZE, 2 * MIXER_WIDTH),
                               lambda b, p, pt: (pt[b * n_pages + n_pages - 1 - p], 0, 0)),
                  pl.BlockSpec(tri.shape, lambda b, p, pt: (0, 0))],
        out_specs=pl.BlockSpec((1, 1, MIXER_WIDTH), lambda b, p, pt: (b, 0, 0)),
        scratch_shapes=[pltpu.VMEM((SAMPLE_ROWS, 1), F32), pltpu.VMEM((SAMPLE_ROWS, MIXER_WIDTH), F32)],
    )
    return pl.pallas_call(
        _sb_sample_kernel,
        grid_spec=grid_spec,
        out_shape=jax.ShapeDtypeStruct((s, 1, MIXER_WIDTH), F32),
        compiler_params=_params("parallel", "arbitrary"),
        name="sb_attn_sample",
    )(page_table.reshape(-1), q, pool, tri)


def _rope_tables(positions):
    half = ROT_DIM // 2
    inv = np.exp(-math.log(ROPE_THETA) * np.arange(0, ROT_DIM, 2, dtype=np.float64) / ROT_DIM)
    ang = np.asarray(positions, np.float64)[:, None] * inv[None, :]
    lane = np.arange(LANES) % HEAD_DIM
    cos = np.ones((len(positions), LANES))
    sa = np.zeros((len(positions), LANES))
    sb = np.zeros((len(positions), LANES))
    for l in range(LANES):
        r = lane[l]
        if r < half:
            cos[:, l] = np.cos(ang[:, r])
            sb[:, l] = -np.sin(ang[:, r])
        elif r < ROT_DIM:
            cos[:, l] = np.cos(ang[:, r - half])
            sa[:, l] = np.sin(ang[:, r - half])
    return tuple(jnp.asarray(a, F32) for a in (cos, sa, sb))


def _segment_ones():
    lane = np.arange(LANES)
    return jnp.asarray(lane[:, None] // HEAD_DIM == lane[None, :] // HEAD_DIM, BF16)


def _gate_expand():
    e = np.zeros((3, LANES, MIXER_WIDTH), np.float32)
    for j in range(3):
        for h in range(NSA_HEADS):
            e[j, 3 * h + j, h * HEAD_DIM:(h + 1) * HEAD_DIM] = 1.0
    return jnp.asarray(e, BF16)


def _later_ones(n):
    idx = np.arange(n)
    return jnp.asarray(idx[:, None] > idx[None, :], BF16)


def _compress_weights(w_cmp):
    w = w_cmp.reshape(2, NSA_BLOCK, HEAD_DIM, HEAD_DIM)
    out = jnp.zeros((2, NSA_BLOCK, NSA_GROUPS, HEAD_DIM, NSA_GROUPS, HEAD_DIM), F32)
    for g in range(NSA_GROUPS):
        out = out.at[:, :, g, :, g, :].set(w)
    return out.reshape(2, NSA_BLOCK, LANES, LANES).astype(BF16)


def kernel(x_prompt, x_sample, cache_nsa_kv, state_nsa_win, state_conv, cache_sb_kv, cache_mem_kv, page_table,
           mem_prompt, norm_mix, norm_ffn, norm_mem, w_in_nsa, w_in_conv, w_in_sb, w_out, w_mem_kv, qk_norm_nsa,
           qk_norm_mem, w_nsa_cmp, conv_w, conv_b, conv_ln_g, conv_ln_b, w_ffn_in, w_ffn_out):
    n_batch, seq, d_model = x_prompt.shape
    n_dec = x_sample.shape[0]
    depth = w_out.shape[0]
    n_pages = page_table.shape[1]
    past = n_pages * PAGE_SIZE
    n_mem = mem_prompt.shape[1]
    ffn_hidden = w_ffn_out.shape[1]
    assert x_sample.shape[1] == 1 and seq % CONV_TILE == 0 and seq >= NSA_WINDOW + Q_TILE
    assert past % NSA_BLOCK == 0 and ffn_hidden % FFN_CHUNK == 0

    xp = x_prompt.reshape(n_batch * seq, d_model)
    xs = x_sample.reshape(n_dec, d_model)

    seg = _segment_ones()
    gexp = _gate_expand()
    tri = _later_ones(SB_CHUNK)
    rope_p = _rope_tables(np.arange(seq))
    rope_s = _rope_tables(np.full((n_dec,), past))

    nq, nkv = MIXER_WIDTH, 6 * NSA_GROUPS * HEAD_DIM
    n_gate = 3 * NSA_HEADS
    w_nsa = jnp.concatenate(
        [w_in_nsa[:, :, :nq + nkv], w_in_nsa[:, :, nq + nkv + n_gate:], w_in_nsa[:, :, nq + nkv:nq + nkv + n_gate],
         jnp.zeros(w_in_nsa.shape[:2] + (LANES - n_gate,), F32)], axis=-1).astype(BF16)
    w_conv = w_in_conv.astype(BF16)
    w_sb = w_in_sb.astype(BF16)
    w_o1 = w_out[:, :MIXER_WIDTH].astype(BF16)
    w_o2 = w_out[:, MIXER_WIDTH:].astype(BF16)
    n_ch = ffn_hidden // FFN_CHUNK
    w_gate = w_ffn_in[:, :, :ffn_hidden].reshape(depth, d_model, n_ch, FFN_CHUNK).transpose(0, 2, 1, 3).astype(BF16)
    w_up = w_ffn_in[:, :, ffn_hidden:].reshape(depth, d_model, n_ch, FFN_CHUNK).transpose(0, 2, 1, 3).astype(BF16)
    w_down = w_ffn_out.reshape(depth, n_ch, FFN_CHUNK, d_model).astype(BF16)

    gq_mem = jnp.tile(qk_norm_mem[:, 0], (1, LANES // HEAD_DIM))
    gk_mem = jnp.tile(qk_norm_mem[:, 1], (1, LANES // HEAD_DIM))
    mem_kv = mem_kv_all(mem_prompt.reshape(n_batch * n_mem, d_model), norm_mem, w_mem_kv.astype(BF16), gk_mem, seg)

    nsa_p, nsa_s, win_p, win_s, conv_p, conv_s, sb_p, sb_s = [], [], [], [], [], [], [], []
    for layer in range(depth):
        kind, j = layer % N_MIXERS, layer // N_MIXERS
        if kind == 0:
            up = proj_in(xp, norm_mix[layer], w_nsa[j])
            us = proj_in(xs, norm_mix[layer], w_nsa[j])
            mem_col = 2 * MIXER_WIDTH // MEM_WIDTH
            gq = jnp.tile(qk_norm_nsa[j, 0], LANES // HEAD_DIM).reshape(1, LANES)
            gkv = jnp.tile(qk_norm_nsa[j, 1:4], (1, LANES // HEAD_DIM)).reshape(3, 1, LANES)
            wbd = _compress_weights(w_nsa_cmp[j])

            q_p, cs_p, wn_p, gt_p = nsa_prep(up, rope_p, gq, gkv, seg, seq // 256, 256)
            cmp_p = compress(cs_p, wbd)
            o_p = nsa_attn_prompt(q_p, gt_p, cs_p, wn_p, cmp_p, gexp, n_batch)

            q_s, cs_s, wn_s, gt_s = nsa_prep(us, rope_s, gq, gkv, seg, 1, n_dec)
            pool = cache_nsa_kv[j]
            n_pool = pool.shape[0]
            pool2 = pool.reshape(n_pool * PAGE_SIZE, 4 * LANES)
            cmp_pool = compress(pool2, wbd).reshape(n_pool, 4 * LANES)
            qz, ocmp, mask = nsa_sample_cmp(page_table, q_s.reshape(n_dec, 1, MIXER_WIDTH), cmp_pool, past)
            oslc = nsa_sample_sel(page_table, qz, mask, pool.reshape(n_pool, PAGE_SIZE, 4 * LANES),
                                  cs_s.reshape(n_dec, 1, 4 * LANES))
            win_state = state_nsa_win[j].reshape(n_dec, -1, 2 * LANES)
            o_s = nsa_sample_win(qz, win_state, wn_s.reshape(n_dec, 1, 2 * LANES), ocmp, oslc,
                                 gt_s.reshape(n_dec, 1, LANES), gexp).reshape(n_dec, MIXER_WIDTH)

            nsa_p.append(cs_p.reshape(n_batch, seq, 4, NSA_GROUPS, HEAD_DIM))
            nsa_s.append(cs_s.reshape(n_dec, 1, 4, NSA_GROUPS, HEAD_DIM))
            win_p.append(wn_p.reshape(n_batch, seq, 2, NSA_GROUPS, HEAD_DIM)[:, seq - min(NSA_WINDOW, seq):])
            win_s.append(jnp.concatenate([win_state[:, 1:], wn_s.reshape(n_dec, 1, 2 * LANES)], axis=1)
                         .reshape(state_nsa_win.shape[1:]))
        elif kind == 1:
            up = proj_in(xp, norm_mix[layer], w_conv[j])
            us = proj_in(xs, norm_mix[layer], w_conv[j])
            mem_col = 2 * MIXER_WIDTH // MEM_WIDTH
            cw = jnp.concatenate([conv_w[j], jnp.zeros((CONV_HALO - CONV_WIDTH, MIXER_WIDTH), F32)], axis=0)
            vecs = [v[j].reshape(1, MIXER_WIDTH) for v in (conv_b, conv_ln_g, conv_ln_b)]
            o_p, tail = conv_prompt(up, cw, *vecs, n_batch)
            o_s, glu_s = conv_sample(us, state_conv[j].transpose(1, 0, 2), cw, *vecs)
            conv_p.append(tail[:, CONV_HALO - (CONV_WIDTH - 1):])
            conv_s.append(jnp.concatenate([state_conv[j][:, 1:], glu_s[:, None]], axis=1))
        else:
            up = proj_in(xp, norm_mix[layer], w_sb[j])
            us = proj_in(xs, norm_mix[layer], w_sb[j])
            mem_col = 3 * MIXER_WIDTH // MEM_WIDTH
            o_p = sb_attn_prompt(up, tri, n_batch)
            pool = cache_sb_kv[j]
            o_s = sb_attn_sample(page_table, us[:, :MIXER_WIDTH].reshape(n_dec, 1, MIXER_WIDTH),
                                 pool.reshape(pool.shape[0], PAGE_SIZE, 2 * MIXER_WIDTH), tri
                                 ).reshape(n_dec, MIXER_WIDTH)
            sb_p.append(up[:, MIXER_WIDTH:3 * MIXER_WIDTH].reshape(n_batch, seq, 2, NSA_HEADS, HEAD_DIM))
            sb_s.append(us[:, MIXER_WIDTH:3 * MIXER_WIDTH].reshape(n_dec, 1, 2, NSA_HEADS, HEAD_DIM))

        gq_l = gq_mem[layer].reshape(1, LANES)
        om_p = mem_attn_prompt(up, mem_col, mem_kv[layer], gq_l, seg, n_batch)
        qm_s = us[:, mem_col * MEM_WIDTH:(mem_col + 1) * MEM_WIDTH].reshape(n_dec, 1, MEM_WIDTH)
        om_s = mem_attn_sample(qm_s, cache_mem_kv[layer].reshape(n_dec, -1, 2 * MEM_WIDTH), gq_l, seg
                               ).reshape(n_dec, MEM_WIDTH)

        ffn_w = (w_o1[layer], w_o2[layer], norm_ffn[layer], w_gate[layer], w_up[layer], w_down[layer])
        xp = out_ffn(o_p, om_p, xp, *ffn_w)
        xs = out_ffn(o_s, om_s, xs, *ffn_w)

    return (xp.reshape(x_prompt.shape), xs.reshape(x_sample.shape),
            jnp.stack(nsa_p), jnp.stack(nsa_s), jnp.stack(win_p), jnp.stack(win_s),
            jnp.stack(conv_p), jnp.stack(conv_s), jnp.stack(sb_p), jnp.stack(sb_s),
            mem_kv.reshape(depth, n_batch, n_mem, 2, MEM_HEADS, HEAD_DIM))
```

```python
import functools
import math

import numpy as np
import jax
import jax.numpy as jnp
from jax import lax
from jax.experimental import pallas as pl
from jax.experimental.pallas import tpu as pltpu

F32 = jnp.float32
BF16 = jnp.bfloat16

HEAD_DIM = 64
MEM_HEADS = 4
MEM_WIDTH = MEM_HEADS * HEAD_DIM
N_MIXERS = 3
NSA_HEADS = 12
NSA_GROUPS = 2
NSA_HPG = NSA_HEADS // NSA_GROUPS
NSA_BLOCK = 64
NSA_TOPK = 16
NSA_WINDOW = 512
MIXER_WIDTH = NSA_HEADS * HEAD_DIM
CONV_WIDTH = 31
ROPE_THETA = 500000.0
ROT_DIM = HEAD_DIM // 4
EPS = 1e-6
PAGE_SIZE = 128

LANES = 128
SUBLANES = 8
VMEM_LIMIT = 56 * 1024 * 1024

ROW_TILE = 512
FFN_CHUNK = 256
Q_TILE = 128
SEL_CHUNK = 512
SB_CHUNK = 128
CONV_TILE = 512
CMP_BLOCKS = 256
CMP_PAGES = 128
MEM_BATCH = 8
CONV_HALO = 32

SCORE_SCALE = HEAD_DIM ** -0.5
BIG = 1e30
UNDERFLOW_LOG = -104.0
NT_DIMS = (((1,), (1,)), ((), ()))


def _params(*sem):
    return pltpu.CompilerParams(dimension_semantics=sem, vmem_limit_bytes=VMEM_LIMIT)


def _pick_tile(n, cap):
    return max(t for t in range(SUBLANES, cap + 1, SUBLANES) if n % t == 0)


def _iota(shape, dim):
    return lax.broadcasted_iota(jnp.int32, shape, dim)


def _dot(a, b):
    return jnp.dot(a.astype(BF16), b.astype(BF16), preferred_element_type=F32)


def _dot_nt(a, b):
    return lax.dot_general(a.astype(BF16), b.astype(BF16), NT_DIMS, preferred_element_type=F32)


def _split2(x):
    hi = x.astype(BF16)
    lo = (x - hi.astype(F32)).astype(BF16)
    return hi, lo


def _split3(x):
    hi = x.astype(BF16)
    r = x - hi.astype(F32)
    mid = r.astype(BF16)
    lo = (r - mid.astype(F32)).astype(BF16)
    return hi, mid, lo


def _dot_x2(x, w_bf16):
    return sum(jnp.dot(part, w_bf16, preferred_element_type=F32) for part in _split2(x))


def _dot_x3(x, w_bf16):
    return sum(jnp.dot(part, w_bf16, preferred_element_type=F32) for part in _split3(x))


def _dot_w3(w_bf16, x):
    return sum(jnp.dot(w_bf16, part, preferred_element_type=F32) for part in _split3(x))


def _dot_nt_precise(a, b):
    ah, al = _split2(a)
    bh, bl = _split2(b)
    return (lax.dot_general(ah, bh, NT_DIMS, preferred_element_type=F32)
            + lax.dot_general(ah, bl, NT_DIMS, preferred_element_type=F32)
            + lax.dot_general(al, bh, NT_DIMS, preferred_element_type=F32))


def _bf16_round(x):
    return x.astype(BF16).astype(F32)


def _rms_rows(x, g):
    ms = jnp.mean(x * x, axis=-1, keepdims=True)
    return x * lax.rsqrt(ms + EPS) * g


def _head_rms(x, seg_ones, g):
    ms = _dot_x2(x * x, seg_ones) * (1.0 / HEAD_DIM)
    return x * lax.rsqrt(ms + EPS) * g


def _sigmoid(x):
    return 1.0 / (1.0 + jnp.exp(-x))


def _softplus(z):
    return jnp.maximum(z, 0.0) + jnp.log1p(jnp.exp(-jnp.abs(z)))


def _proj_in_kernel(x_ref, g_ref, w_ref, o_ref, *narrow_ref, narrow_cols):
    h = _rms_rows(x_ref[...], g_ref[...]).astype(BF16)
    u = jnp.dot(h, w_ref[...], preferred_element_type=F32)
    o_ref[...] = u
    if narrow_cols is not None:
        narrow_ref[0][...] = u[:, narrow_cols[0]:narrow_cols[1]].astype(BF16)


def proj_in(x, g, w, narrow_cols=None):
    n, d = x.shape
    m = w.shape[1]
    tm = min(ROW_TILE, n)
    out_specs = [pl.BlockSpec((tm, m), lambda i: (i, 0))]
    out_shape = [jax.ShapeDtypeStruct((n, m), F32)]
    if narrow_cols is not None:
        width = narrow_cols[1] - narrow_cols[0]
        out_specs.append(pl.BlockSpec((tm, width), lambda i: (i, 0)))
        out_shape.append(jax.ShapeDtypeStruct((n, width), BF16))
    res = pl.pallas_call(
        functools.partial(_proj_in_kernel, narrow_cols=narrow_cols),
        grid=(n // tm,),
        in_specs=[pl.BlockSpec((tm, d), lambda i: (i, 0)),
                  pl.BlockSpec((1, d), lambda i: (0, 0)),
                  pl.BlockSpec((d, m), lambda i: (0, 0))],
        out_specs=out_specs,
        out_shape=out_shape,
        compiler_params=_params("parallel"),
        name="proj_in",
    )(x, g.reshape(1, d), w)
    return res if narrow_cols is not None else res[0]


def _out_ffn_kernel(o_ref, om_ref, x_ref, w1_ref, w2_ref, gf_ref, wg_ref, wu_ref, wo_ref, y_ref):
    x = x_ref[...] + _dot(o_ref[...], w1_ref[...]) + _dot(om_ref[...], w2_ref[...])
    h = _rms_rows(x, gf_ref[...]).astype(BF16)

    def chunk(c, acc):
        gate = jnp.dot(h, wg_ref[c], preferred_element_type=F32)
        up = jnp.dot(h, wu_ref[c], preferred_element_type=F32)
        act = (gate * _sigmoid(gate) * up).astype(BF16)
        return acc + jnp.dot(act, wo_ref[c], preferred_element_type=F32)

    y_ref[...] = lax.fori_loop(0, wg_ref.shape[0], chunk, x)


def out_ffn(o, om, x, w1, w2, gf, wg, wu, wo):
    n, d = x.shape
    tm = min(ROW_TILE, n)
    const = lambda *shape: pl.BlockSpec(shape, lambda i: (0,) * len(shape))
    return pl.pallas_call(
        _out_ffn_kernel,
        grid=(n // tm,),
        in_specs=[pl.BlockSpec((tm, o.shape[1]), lambda i: (i, 0)),
                  pl.BlockSpec((tm, om.shape[1]), lambda i: (i, 0)),
                  pl.BlockSpec((tm, d), lambda i: (i, 0)),
                  const(*w1.shape), const(*w2.shape), const(1, d),
                  const(*wg.shape), const(*wu.shape), const(*wo.shape)],
        out_specs=pl.BlockSpec((tm, d), lambda i: (i, 0)),
        out_shape=jax.ShapeDtypeStruct((n, d), F32),
        compiler_params=_params("parallel"),
        name="out_ffn",
    )(o, om, x, w1, w2, gf.reshape(1, d), wg, wu, wo)


def _mem_kv_kernel(mem_ref, gn_ref, w_ref, gk_ref, seg_ref, o_ref):
    h = _rms_rows(mem_ref[...], gn_ref[0]).astype(BF16)
    kv = jnp.dot(h, w_ref[0], preferred_element_type=F32)
    seg = seg_ref[...]
    for c in range(MEM_WIDTH // LANES):
        sl = slice(c * LANES, (c + 1) * LANES)
        o_ref[0, :, sl] = _head_rms(kv[:, sl], seg, gk_ref[0])
    o_ref[0, :, MEM_WIDTH:] = kv[:, MEM_WIDTH:]


def mem_kv_all(mem, g_norm, w_kv, g_k, seg):
    n, d = mem.shape
    nl = w_kv.shape[0]
    return pl.pallas_call(
        _mem_kv_kernel,
        grid=(nl,),
        in_specs=[pl.BlockSpec((n, d), lambda l: (0, 0)),
                  pl.BlockSpec((1, 1, d), lambda l: (l, 0, 0)),
                  pl.BlockSpec((1, d, 2 * MEM_WIDTH), lambda l: (l, 0, 0)),
                  pl.BlockSpec((1, 1, LANES), lambda l: (l, 0, 0)),
                  pl.BlockSpec((LANES, LANES), lambda l: (0, 0))],
        out_specs=pl.BlockSpec((1, n, 2 * MEM_WIDTH), lambda l: (l, 0, 0)),
        out_shape=jax.ShapeDtypeStruct((nl, n, 2 * MEM_WIDTH), F32),
        compiler_params=_params("arbitrary"),
        name="mem_kv",
    )(mem, g_norm.reshape(nl, 1, d), w_kv, g_k.reshape(nl, 1, LANES), seg)


def _mem_query(qm, seg, gq):
    parts = [_head_rms(qm[:, c * LANES:(c + 1) * LANES], seg, gq) for c in range(MEM_WIDTH // LANES)]
    return jnp.concatenate(parts, axis=1) * SCORE_SCALE


def _softmax_rows(s):
    p = jnp.exp(s - jnp.max(s, axis=-1, keepdims=True))
    return p / jnp.sum(p, axis=-1, keepdims=True)


def _mem_attn_kernel(qm_ref, kv_ref, gq_ref, seg_ref, o_ref):
    qn = _mem_query(qm_ref[...], seg_ref[...], gq_ref[...])
    r = qn.shape[0]
    head_of_lane = _iota((1, MEM_WIDTH), 1) >> 6
    qz = jnp.concatenate([jnp.where(head_of_lane == h, qn, 0.0) for h in range(MEM_HEADS)], axis=0)
    p = _softmax_rows(_dot_nt(qz, kv_ref[:, :MEM_WIDTH]))
    res = _dot(p, kv_ref[:, MEM_WIDTH:])
    out = jnp.zeros((r, MEM_WIDTH), F32)
    for h in range(MEM_HEADS):
        out = out + jnp.where(head_of_lane == h, res[h * r:(h + 1) * r], 0.0)
    o_ref[...] = out


def mem_attn_prompt(u, mem_col, kv, gq, seg, n_batch):
    n = u.shape[0]
    t = n // n_batch
    tm = 256
    n_mem = kv.shape[0] // n_batch
    nt = t // tm
    return pl.pallas_call(
        _mem_attn_kernel,
        grid=(n_batch, nt),
        in_specs=[pl.BlockSpec((tm, MEM_WIDTH), lambda b, i: (b * nt + i, mem_col)),
                  pl.BlockSpec((n_mem, 2 * MEM_WIDTH), lambda b, i: (b, 0)),
                  pl.BlockSpec((1, LANES), lambda b, i: (0, 0)),
                  pl.BlockSpec((LANES, LANES), lambda b, i: (0, 0))],
        out_specs=pl.BlockSpec((tm, MEM_WIDTH), lambda b, i: (b * nt + i, 0)),
        out_shape=jax.ShapeDtypeStruct((n, MEM_WIDTH), F32),
        compiler_params=_params("parallel", "parallel"),
        name="mem_attn_prompt",
    )(u, kv, gq, seg)


def _mem_attn_sample_kernel(qm_ref, kv_ref, gq_ref, seg_ref, o_ref):
    qn = _mem_query(qm_ref[0], seg_ref[...], gq_ref[...])
    row = _iota((SUBLANES, 1), 0)
    head_of_lane = _iota((1, MEM_WIDTH), 1) >> 6
    out = jnp.zeros((MEM_BATCH, MEM_WIDTH), F32)
    for i in range(MEM_BATCH):
        qz = jnp.where(head_of_lane == (row & (MEM_HEADS - 1)), jnp.broadcast_to(qn[i:i + 1], (SUBLANES, MEM_WIDTH)), 0.0)
        p = _softmax_rows(_dot(qz, kv_ref[i, 0:MEM_WIDTH, :]))
        res = _dot_nt(p, kv_ref[i, MEM_WIDTH:2 * MEM_WIDTH, :])
        flat = jnp.sum(jnp.where(head_of_lane == row, res, 0.0), axis=0, keepdims=True)
        out = out + jnp.where(_iota((MEM_BATCH, 1), 0) == i, flat, 0.0)
    o_ref[0] = out


def mem_attn_sample(qm, kv_t, layer, gq, seg):
    _, s, _, n_mem = kv_t.shape
    nb = s // MEM_BATCH
    return pl.pallas_call(
        _mem_attn_sample_kernel,
        grid=(nb,),
        in_specs=[pl.BlockSpec((1, MEM_BATCH, MEM_WIDTH), lambda b: (b, 0, 0)),
                  pl.BlockSpec((None, MEM_BATCH, 2 * MEM_WIDTH, n_mem), lambda b: (layer, b, 0, 0)),
                  pl.BlockSpec((1, LANES), lambda b: (0, 0)),
                  pl.BlockSpec((LANES, LANES), lambda b: (0, 0))],
        out_specs=pl.BlockSpec((1, MEM_BATCH, MEM_WIDTH), lambda b: (b, 0, 0)),
        out_shape=jax.ShapeDtypeStruct((nb, MEM_BATCH, MEM_WIDTH), F32),
        compiler_params=_params("parallel"),
        name="mem_attn_sample",
    )(qm.reshape(nb, MEM_BATCH, MEM_WIDTH), kv_t, gq, seg).reshape(s, MEM_WIDTH)


def _nsa_prep_kernel(q_ref, kv_ref, gt_ref, cos_ref, sa_ref, sb_ref, gq_ref, gkv_ref, seg_ref,
                     qo_ref, cso_ref, wino_ref, go_ref):
    cos, sa, sb = cos_ref[...], sa_ref[...], sb_ref[...]
    seg = seg_ref[...]
    half = ROT_DIM // 2

    def norm_rope(x, g):
        y = _head_rms(x, seg, g)
        return y * cos + pltpu.roll(y, half, 1) * sa + pltpu.roll(y, LANES - half, 1) * sb

    for c in range(MIXER_WIDTH // LANES):
        sl = slice(c * LANES, (c + 1) * LANES)
        qo_ref[:, sl] = norm_rope(q_ref[:, sl], gq_ref[...])
    for c in range(6):
        x = kv_ref[:, c * LANES:(c + 1) * LANES]
        if c % 2 == 0:
            x = norm_rope(x, gkv_ref[c // 2])
        if c < 4:
            cso_ref[:, c * LANES:(c + 1) * LANES] = x
        else:
            wino_ref[:, (c - 4) * LANES:(c - 3) * LANES] = x
    go_ref[...] = _sigmoid(gt_ref[...])


def nsa_prep(u, rope_tabs, gq, gkv, seg, n_time_tiles, tm):
    n = u.shape[0]
    cos, sa, sb = rope_tabs
    tab = pl.BlockSpec((tm, LANES), lambda i: (i % n_time_tiles, 0))
    const = lambda *shape: pl.BlockSpec(shape, lambda i: (0,) * len(shape))
    return pl.pallas_call(
        _nsa_prep_kernel,
        grid=(n // tm,),
        in_specs=[pl.BlockSpec((tm, MIXER_WIDTH), lambda i: (i, 0)),
                  pl.BlockSpec((tm, MIXER_WIDTH), lambda i: (i, 1)),
                  pl.BlockSpec((tm, LANES), lambda i: (i, (2 * MIXER_WIDTH + MEM_WIDTH) // LANES)),
                  tab, tab, tab, const(1, LANES), const(3, 1, LANES), const(LANES, LANES)],
        out_specs=[pl.BlockSpec((tm, MIXER_WIDTH), lambda i: (i, 0)),
                   pl.BlockSpec((tm, 4 * LANES), lambda i: (i, 0)),
                   pl.BlockSpec((tm, 2 * LANES), lambda i: (i, 0)),
                   pl.BlockSpec((tm, LANES), lambda i: (i, 0))],
        out_shape=[jax.ShapeDtypeStruct((n, MIXER_WIDTH), F32),
                   jax.ShapeDtypeStruct((n, 4 * LANES), F32),
                   jax.ShapeDtypeStruct((n, 2 * LANES), F32),
                   jax.ShapeDtypeStruct((n, LANES), F32)],
        compiler_params=_params("parallel"),
        name="nsa_prep",
    )(u, u, u, cos, sa, sb, gq, gkv, seg)


def _compress_kernel(xk_ref, xv_ref, w_ref, o_ref):
    mb = o_ref.shape[0]
    for c, x_ref in enumerate((xk_ref, xv_ref)):
        acc = jnp.zeros((mb, LANES), F32)
        for r in range(NSA_BLOCK):
            xr = x_ref[pl.ds(r, mb, stride=NSA_BLOCK), :]
            acc = acc + jnp.dot(xr.astype(BF16), w_ref[c, r], preferred_element_type=F32)
        o_ref[:, c * LANES:(c + 1) * LANES] = acc


def compress(cs, wbd):
    nblk = cs.shape[0] // NSA_BLOCK
    mb = _pick_tile(nblk, CMP_BLOCKS)
    return pl.pallas_call(
        _compress_kernel,
        grid=(nblk // mb,),
        in_specs=[pl.BlockSpec((mb * NSA_BLOCK, LANES), lambda i: (i, 0)),
                  pl.BlockSpec((mb * NSA_BLOCK, LANES), lambda i: (i, 1)),
                  pl.BlockSpec(wbd.shape, lambda i: (0, 0, 0, 0))],
        out_specs=pl.BlockSpec((mb, 2 * LANES), lambda i: (i, 0)),
        out_shape=jax.ShapeDtypeStruct((nblk, 2 * LANES), F32),
        compiler_params=_params("parallel"),
        name="nsa_compress",
    )(cs, cs, wbd)


def _compress_pool_kernel(x_ref, w_ref, o_ref):
    mbp = o_ref.shape[0]
    low = _iota((1, LANES), 1) < HEAD_DIM
    for c in range(2):
        slabs = []
        for g in range(NSA_GROUPS):
            acc = jnp.zeros((mbp, LANES), F32)
            for d in range(HEAD_DIM):
                xr = x_ref[:, c * LANES + g * HEAD_DIM + d, :]
                acc = acc + jnp.dot(xr.astype(BF16), w_ref[c, d], preferred_element_type=F32)
            slabs.append(acc)
        first = jnp.where(low, slabs[0], pltpu.roll(slabs[1], HEAD_DIM, 1))
        second = jnp.where(low, pltpu.roll(slabs[0], HEAD_DIM, 1), slabs[1])
        o_ref[:, c * LANES:(c + 1) * LANES] = first
        o_ref[:, (2 + c) * LANES:(3 + c) * LANES] = second


def compress_pool(pool_t, layer, wt):
    n_pool = pool_t.shape[1]
    mbp = _pick_tile(n_pool, CMP_PAGES)
    return pl.pallas_call(
        _compress_pool_kernel,
        grid=(n_pool // mbp,),
        in_specs=[pl.BlockSpec((None, mbp, 2 * LANES, PAGE_SIZE), lambda i: (layer, i, 0, 0)),
                  pl.BlockSpec(wt.shape, lambda i: (0, 0, 0, 0))],
        out_specs=pl.BlockSpec((mbp, 4 * LANES), lambda i: (i, 0)),
        out_shape=jax.ShapeDtypeStruct((n_pool, 4 * LANES), F32),
        compiler_params=_params("parallel"),
        name="nsa_compress_pool",
    )(pool_t, wt)


def _group_queries(q_chunks, g):
    parts = []
    for h in range(NSA_HPG):
        hh = g * NSA_HPG + h
        chunk = q_chunks(hh // 2)
        if hh % 2 != g:
            chunk = pltpu.roll(chunk, HEAD_DIM, 1)
        parts.append(chunk)
    qz = jnp.concatenate(parts, axis=0) * SCORE_SCALE
    return jnp.where((_iota((1, LANES), 1) >> 6) == g, qz, 0.0)


def _select_blocks(imp, cur):
    r, nb = imp.shape
    nidx = _iota((1, nb), 1)
    score = jnp.where(nidx == cur, BIG, jnp.where(nidx < cur, imp, -1.0))
    rank = jnp.zeros((r, nb), F32)
    for m in range(nb):
        col = score[:, m:m + 1]
        rank = rank + jnp.where(nidx > m, jnp.where(col >= score, 1.0, 0.0), jnp.where(col > score, 1.0, 0.0))
    return (rank < float(NSA_TOPK)) & (nidx <= cur)


def _masked_softmax_parts(s, ok):
    m = jnp.max(jnp.where(ok, s, -BIG), axis=-1, keepdims=True)
    p = jnp.where(ok, jnp.exp(s - m), 0.0)
    return p, jnp.maximum(jnp.sum(p, axis=-1, keepdims=True), 1e-30)


def _place_heads(res, g, rows):
    low = _iota((1, LANES), 1) < HEAD_DIM
    out = []
    for c in range(NSA_HPG // 2):
        a = res[(2 * c) * rows:(2 * c + 1) * rows]
        b = res[(2 * c + 1) * rows:(2 * c + 2) * rows]
        if g == 1:
            a = pltpu.roll(a, HEAD_DIM, 1)
        else:
            b = pltpu.roll(b, HEAD_DIM, 1)
        out.append(jnp.where(low, a, b))
    return out


def _nsa_attn_kernel(q_ref, g_ref, cs_ref, win_ref, cmp_ref, gexp_ref, o_ref):
    qt = pl.program_id(1)
    t0 = qt * Q_TILE
    seq = cs_ref.shape[0]
    nb = cmp_ref.shape[0]
    tpos = t0 + _iota((Q_TILE, 1), 0)
    cur = tpos >> 6
    rows = NSA_HPG * Q_TILE

    gates = g_ref[...]
    gexp = [_dot_x2(gates, gexp_ref[j]) for j in range(3)]

    win_span = min(NSA_WINDOW + Q_TILE, seq)
    w0 = pl.multiple_of(jnp.maximum(t0 + Q_TILE - win_span, 0), Q_TILE)
    n_chunks = (t0 + Q_TILE + SEL_CHUNK - 1) // SEL_CHUNK

    branch_out = []
    for g in range(NSA_GROUPS):
        qz = _group_queries(lambda c: q_ref[:, c * LANES:(c + 1) * LANES], g)

        s = _dot_nt_precise(qz, cmp_ref[:, 0:LANES]).reshape(NSA_HPG, Q_TILE, nb)
        nidx = _iota((1, nb), 1)
        complete = ((nidx + 1) * NSA_BLOCK <= tpos + 1)[None]
        p, den = _masked_softmax_parts(s, complete)
        p_cmp = p / den
        o_cmp = _dot(p_cmp.reshape(rows, nb), cmp_ref[:, LANES:2 * LANES])
        sel = _select_blocks(jnp.sum(p_cmp, axis=0), cur)
        sel_bf = jnp.where(sel, 1.0, 0.0).astype(BF16)

        def sel_chunk(kc, carry):
            m, l, acc = carry
            k0 = pl.multiple_of(kc * SEL_CHUNK, SEL_CHUNK)
            kch = cs_ref[pl.ds(k0, SEL_CHUNK), 2 * LANES:3 * LANES]
            vch = cs_ref[pl.ds(k0, SEL_CHUNK), 3 * LANES:4 * LANES]
            sc = _dot_nt(qz, kch).reshape(NSA_HPG, Q_TILE, SEL_CHUNK)
            kpos = k0 + _iota((1, SEL_CHUNK), 1)
            expand = jnp.where((kpos >> 6) == _iota((nb, 1), 0), 1.0, 0.0).astype(BF16)
            chosen = jnp.dot(sel_bf, expand, preferred_element_type=F32)
            ok = ((chosen > 0.5) & (kpos <= tpos))[None]
            m_new = jnp.maximum(m, jnp.max(jnp.where(ok, sc, -BIG), axis=-1, keepdims=True))
            alpha = jnp.exp(m - m_new)
            pp = jnp.where(ok, jnp.exp(sc - m_new), 0.0)
            l = alpha * l + jnp.sum(pp, axis=-1, keepdims=True)
            pv = _dot(pp.reshape(rows, SEL_CHUNK), vch).reshape(NSA_HPG, Q_TILE, LANES)
            return m_new, l, alpha * acc + pv

        init = (jnp.full((NSA_HPG, Q_TILE, 1), -BIG, F32), jnp.zeros((NSA_HPG, Q_TILE, 1), F32),
                jnp.zeros((NSA_HPG, Q_TILE, LANES), F32))
        _, l, acc = lax.fori_loop(0, n_chunks, sel_chunk, init)
        o_slc = (acc / jnp.maximum(l, 1e-30)).reshape(rows, LANES)

        kw = win_ref[pl.ds(w0, win_span), 0:LANES]
        vw = win_ref[pl.ds(w0, win_span), LANES:2 * LANES]
        sw = _dot_nt(qz, kw).reshape(NSA_HPG, Q_TILE, win_span)
        wpos = w0 + _iota((1, win_span), 1)
        w_ok = ((wpos <= tpos) & (wpos > tpos - NSA_WINDOW))[None]
        pw, denw = _masked_softmax_parts(sw, w_ok)
        o_win = _dot((pw / denw).reshape(rows, win_span), vw)

        branch_out.append([_place_heads(o, g, Q_TILE) for o in (o_cmp, o_slc, o_win)])

    for c in range(MIXER_WIDTH // LANES):
        g, cc = divmod(c, NSA_HPG // 2)
        sl = slice(c * LANES, (c + 1) * LANES)
        o_ref[:, sl] = sum(gexp[j][:, sl] * branch_out[g][j][cc] for j in range(3))


def nsa_attn_prompt(q, gates, cs, win, cmp, gexp, n_batch):
    n = q.shape[0]
    t = n // n_batch
    nt = t // Q_TILE
    nb = cmp.shape[0] // n_batch
    return pl.pallas_call(
        _nsa_attn_kernel,
        grid=(n_batch, nt),
        in_specs=[pl.BlockSpec((Q_TILE, MIXER_WIDTH), lambda b, i: (b * nt + i, 0)),
                  pl.BlockSpec((Q_TILE, LANES), lambda b, i: (b * nt + i, 0)),
                  pl.BlockSpec((t, 4 * LANES), lambda b, i: (b, 0)),
                  pl.BlockSpec((t, 2 * LANES), lambda b, i: (b, 0)),
                  pl.BlockSpec((nb, 2 * LANES), lambda b, i: (b, 0)),
                  pl.BlockSpec(gexp.shape, lambda b, i: (0, 0, 0))],
        out_specs=pl.BlockSpec((Q_TILE, MIXER_WIDTH), lambda b, i: (b * nt + i, 0)),
        out_shape=jax.ShapeDtypeStruct((n, MIXER_WIDTH), F32),
        compiler_params=_params("parallel", "parallel"),
        name="nsa_attn_prompt",
    )(q, gates, cs, win, cmp, gexp)


SAMPLE_ROWS = 16
NB_PAD = 64


def _sample_queries(q_row):
    row = _iota((SAMPLE_ROWS, 1), 0)
    lane_group = _iota((1, LANES), 1) >> 6
    qz = jnp.zeros((SAMPLE_ROWS, LANES), F32)
    for hh in range(NSA_HEADS):
        g = hh // NSA_HPG
        chunk = jnp.broadcast_to(q_row[:, (hh // 2) * LANES:(hh // 2 + 1) * LANES], (SAMPLE_ROWS, LANES))
        if hh % 2 != g:
            chunk = pltpu.roll(chunk, HEAD_DIM, 1)
        qz = qz + jnp.where((row == hh) & (lane_group == g), chunk, 0.0)
    return qz * SCORE_SCALE


def _row_to_column(row):
    n = row.shape[1]
    diag = _iota((n, 1), 0) == _iota((1, n), 1)
    return jnp.sum(jnp.where(diag, jnp.broadcast_to(row, (n, n)), 0.0), axis=1, keepdims=True)


def _attend_with_new_key(s, ok, qz, k_new, v_new, weighted_values):
    s_new = jnp.sum(_bf16_round(qz) * _bf16_round(k_new), axis=-1, keepdims=True)
    m_lanes = jnp.max(jnp.where(ok, s, -BIG), axis=-1, keepdims=True)
    if s.ndim == 3:
        m_lanes = jnp.max(m_lanes, axis=0)
    m = jnp.maximum(m_lanes, s_new)
    p = jnp.where(ok, jnp.exp(s - m), 0.0)
    p_new = jnp.exp(s_new - m)
    total = jnp.sum(p, axis=-1, keepdims=True)
    if s.ndim == 3:
        total = jnp.sum(total, axis=0)
    den = total + p_new
    return (weighted_values(p) + _bf16_round(p_new) * _bf16_round(v_new)) / den


def _nsa_sample_kernel(pt_ref, q_ref, cmp_ref, new_ref, win_ref, wnew_ref, g_ref, gexp_ref, *refs, past):
    n_pages = past // PAGE_SIZE
    page_refs = refs[:n_pages]
    o_ref, wout_ref, cm_ref = refs[n_pages:]
    b = pl.program_id(0)
    head = _iota((SAMPLE_ROWS, 1), 0)
    qz = _sample_queries(q_ref[0])
    qzb = qz.astype(BF16)

    cm_ref[...] = jnp.zeros(cm_ref.shape, F32)
    for p in range(n_pages):
        row = cmp_ref[pl.ds(pt_ref[b * n_pages + p], 1), :]
        cm_ref[2 * p:2 * p + 1, :] = row[:, 0:2 * LANES]
        cm_ref[2 * p + 1:2 * p + 2, :] = row[:, 2 * LANES:4 * LANES]
    s = _dot_nt_precise(qz, cm_ref[:, 0:LANES])
    nidx = _iota((1, NB_PAD), 1)
    complete = (nidx + 1) * NSA_BLOCK <= past + 1
    p, den = _masked_softmax_parts(s, complete)
    p_cmp = p / den
    o_cmp = _dot(p_cmp, cm_ref[:, LANES:2 * LANES])

    row8 = _iota((SUBLANES, 1), 0)
    imp = jnp.zeros((SUBLANES, NB_PAD), F32)
    for g in range(NSA_GROUPS):
        in_group = (head >= g * NSA_HPG) & (head < (g + 1) * NSA_HPG)
        imp = imp + jnp.where(row8 == g, jnp.sum(jnp.where(in_group, p_cmp, 0.0), axis=0, keepdims=True), 0.0)
    cur = jnp.full((SUBLANES, 1), past // NSA_BLOCK, jnp.int32)
    sel = jnp.where(_select_blocks(imp, cur), 1.0, 0.0)
    sel_rows = jnp.where(head < NSA_HPG, sel[0:1], sel[1:2]).astype(BF16)
    kpos = _iota((1, past), 1)
    expand = jnp.where((kpos >> 6) == _iota((NB_PAD, 1), 0), 1.0, 0.0).astype(BF16)
    chosen = jnp.dot(sel_rows, expand, preferred_element_type=F32)

    s3 = jnp.concatenate(
        [jnp.dot(qzb, page_refs[p][0:LANES, :].astype(BF16), preferred_element_type=F32) for p in range(n_pages)],
        axis=0).reshape(n_pages, SAMPLE_ROWS, PAGE_SIZE)
    ok3 = jnp.concatenate([chosen[:, p * PAGE_SIZE:(p + 1) * PAGE_SIZE] for p in range(n_pages)],
                          axis=0).reshape(n_pages, SAMPLE_ROWS, PAGE_SIZE) > 0.5

    def sel_values(pp):
        return sum(_dot_nt(pp[p], page_refs[p][LANES:2 * LANES, :]) for p in range(n_pages))

    o_slc = _attend_with_new_key(s3, ok3, qz, new_ref[0][:, 2 * LANES:3 * LANES], new_ref[0][:, 3 * LANES:4 * LANES],
                                 sel_values)

    lw = win_ref.shape[1]
    sw = jnp.dot(qzb, win_ref[0:LANES, :].astype(BF16), preferred_element_type=F32)
    okw = _iota((1, lw), 1) > lw - NSA_WINDOW
    o_win = _attend_with_new_key(sw, okw, qz, wnew_ref[0][:, 0:LANES], wnew_ref[0][:, LANES:2 * LANES],
                                 lambda pp: _dot_nt(pp, win_ref[LANES:2 * LANES, :]))

    gates = jnp.broadcast_to(g_ref[0], (SUBLANES, LANES))
    lane_head = _iota((1, MIXER_WIDTH), 1) >> 6
    in_place = (head & 1) == jnp.where(head >= NSA_HPG, 1, 0)
    out = jnp.zeros((1, MIXER_WIDTH), F32)
    for j, o in enumerate((o_cmp, o_slc, o_win)):
        tiled = jnp.concatenate([o] * (MIXER_WIDTH // LANES), axis=1)
        rolled = jnp.concatenate([pltpu.roll(o, HEAD_DIM, 1)] * (MIXER_WIDTH // LANES), axis=1)
        placed = jnp.where(lane_head == head, jnp.where(in_place, tiled, rolled), 0.0)
        out = out + _dot_x2(gates, gexp_ref[j])[0:1] * jnp.sum(placed, axis=0, keepdims=True)
    o_ref[0] = out

    new_col = jnp.broadcast_to(_row_to_column(wnew_ref[0]), (2 * LANES, LANES))
    last_lane = _iota((1, LANES), 1) == LANES - 1
    for c in range(lw // LANES):
        shifted = pltpu.roll(win_ref[:, c * LANES:(c + 1) * LANES], LANES - 1, 1)
        if (c + 1) * LANES < lw:
            carry_in = pltpu.roll(win_ref[:, (c + 1) * LANES:(c + 2) * LANES], LANES - 1, 1)
        else:
            carry_in = new_col
        wout_ref[:, c * LANES:(c + 1) * LANES] = jnp.where(last_lane, carry_in, shifted)


def nsa_sample(page_table, layer, q, cmp_pool, cs_new, win_t, win_new, gates, gexp, pool_t):
    s, n_pages = page_table.shape
    past = n_pages * PAGE_SIZE
    lw = win_t.shape[-1]
    per_seq = lambda *shape: pl.BlockSpec((1,) + shape, lambda b, pt: (b,) + (0,) * len(shape))
    page_spec = lambda p: pl.BlockSpec((None, None, 2 * LANES, PAGE_SIZE),
                                       lambda b, pt: (layer, pt[b * n_pages + p], 1, 0))
    grid_spec = pltpu.PrefetchScalarGridSpec(
        num_scalar_prefetch=1,
        grid=(s,),
        in_specs=[per_seq(1, MIXER_WIDTH),
                  pl.BlockSpec(cmp_pool.shape, lambda b, pt: (0, 0)),
                  per_seq(1, 4 * LANES),
                  pl.BlockSpec((None, None, 2 * LANES, lw), lambda b, pt: (layer, b, 0, 0)),
                  per_seq(1, 2 * LANES), per_seq(1, LANES),
                  pl.BlockSpec(gexp.shape, lambda b, pt: (0, 0, 0))]
                 + [page_spec(p) for p in range(n_pages)],
        out_specs=[per_seq(1, MIXER_WIDTH), pl.BlockSpec((None, 2 * LANES, lw), lambda b, pt: (b, 0, 0))],
        scratch_shapes=[pltpu.VMEM((NB_PAD, 2 * LANES), F32)],
    )
    return pl.pallas_call(
        functools.partial(_nsa_sample_kernel, past=past),
        grid_spec=grid_spec,
        out_shape=[jax.ShapeDtypeStruct((s, 1, MIXER_WIDTH), F32),
                   jax.ShapeDtypeStruct((s, 2 * LANES, lw), F32)],
        compiler_params=_params("arbitrary"),
        name="nsa_sample",
    )(page_table.reshape(-1), q, cmp_pool, cs_new, win_t, win_new, gates, gexp, *([pool_t] * n_pages))


def _layer_norm_swish(y, g, b):
    mu = jnp.mean(y, axis=-1, keepdims=True)
    var = jnp.mean(jnp.square(y - mu), axis=-1, keepdims=True)
    z = (y - mu) * lax.rsqrt(var + EPS) * g + b
    return z * _sigmoid(z)


def _conv_prompt_kernel(a_ref, b_ref, ap_ref, bp_ref, w_ref, cb_ref, lg_ref, lb_ref, o_ref, st_ref, buf_ref):
    i = pl.program_id(1)
    glu = a_ref[...] * _sigmoid(b_ref[...])
    prev = ap_ref[...] * _sigmoid(bp_ref[...])
    buf_ref[0:CONV_HALO, :] = jnp.where(i > 0, prev, 0.0)
    buf_ref[CONV_HALO:, :] = glu
    tt = a_ref.shape[0]
    acc = jnp.zeros(glu.shape, F32)
    for k in range(CONV_WIDTH):
        acc = acc + w_ref[k:k + 1, :] * buf_ref[pl.ds(CONV_HALO - (CONV_WIDTH - 1) + k, tt), :]
    o_ref[...] = _layer_norm_swish(acc + cb_ref[...], lg_ref[...], lb_ref[...])

    @pl.when(i == pl.num_programs(1) - 1)
    def _():
        st_ref[0] = glu[tt - CONV_HALO:, :]


def conv_prompt(u, w, cb, lg, lb, n_batch):
    n = u.shape[0]
    t = n // n_batch
    tt = min(CONV_TILE, t)
    nt = t // tt
    ch = MIXER_WIDTH
    ratio = tt // CONV_HALO
    cur = lambda col: pl.BlockSpec((tt, ch), lambda b, i: (b * nt + i, col))
    prev = lambda col: pl.BlockSpec((CONV_HALO, ch), lambda b, i: (jnp.maximum((b * nt + i) * ratio - 1, 0), col))
    const = lambda *shape: pl.BlockSpec(shape, lambda b, i: (0,) * len(shape))
    return pl.pallas_call(
        _conv_prompt_kernel,
        grid=(n_batch, nt),
        in_specs=[cur(0), cur(1), prev(0), prev(1), const(CONV_HALO, ch), const(1, ch), const(1, ch), const(1, ch)],
        out_specs=[pl.BlockSpec((tt, ch), lambda b, i: (b * nt + i, 0)),
                   pl.BlockSpec((1, CONV_HALO, ch), lambda b, i: (b, 0, 0))],
        out_shape=[jax.ShapeDtypeStruct((n, ch), F32),
                   jax.ShapeDtypeStruct((n_batch, CONV_HALO, ch), F32)],
        scratch_shapes=[pltpu.VMEM((CONV_HALO + tt, ch), F32)],
        compiler_params=_params("parallel", "arbitrary"),
        name="conv_prompt",
    )(u, u, u, u, w, cb, lg, lb)


def _conv_sample_kernel(a_ref, b_ref, st_ref, w_ref, cb_ref, lg_ref, lb_ref, o_ref, glu_ref):
    glu = a_ref[...] * _sigmoid(b_ref[...])
    acc = w_ref[CONV_WIDTH - 1:CONV_WIDTH, :] * glu
    for k in range(CONV_WIDTH - 1):
        acc = acc + w_ref[k:k + 1, :] * st_ref[k]
    o_ref[...] = _layer_norm_swish(acc + cb_ref[...], lg_ref[...], lb_ref[...])
    glu_ref[...] = glu


def conv_sample(u, state_t, layer, w, cb, lg, lb):
    s = u.shape[0]
    ch = MIXER_WIDTH
    const = lambda *shape: pl.BlockSpec(shape, lambda i: (0,) * len(shape))
    return pl.pallas_call(
        _conv_sample_kernel,
        grid=(1,),
        in_specs=[pl.BlockSpec((s, ch), lambda i: (0, 0)), pl.BlockSpec((s, ch), lambda i: (0, 1)),
                  pl.BlockSpec((None, CONV_WIDTH - 1, s, ch), lambda i: (layer, 0, 0, 0)),
                  const(CONV_HALO, ch), const(1, ch), const(1, ch), const(1, ch)],
        out_specs=[const(s, ch), const(s, ch)],
        out_shape=[jax.ShapeDtypeStruct((s, ch), F32), jax.ShapeDtypeStruct((s, ch), F32)],
        compiler_params=_params("arbitrary"),
        name="conv_sample",
    )(u, u, state_t, w, cb, lg, lb)


SB_PAIRS = MIXER_WIDTH // LANES


def _stick_weights(z, valid, later, tri):
    sp = _softplus(z)
    log_stay = -sp if valid is None else jnp.where(valid, -sp, 0.0)
    within = _dot_x3(log_stay, tri)
    a = jnp.exp(z - sp + within + later)
    if valid is not None:
        a = jnp.where(valid, a, 0.0)
    return a, within[:, 0:1] + log_stay[:, 0:1]


def _sb_attn_kernel(q_ref, k_ref, v_ref, tri_ref, o_ref, acc_ref):
    qt = pl.program_id(1)
    t0 = qt * Q_TILE
    low = _iota((1, LANES), 1) < HEAD_DIM
    tpos = t0 + (_iota((2 * Q_TILE, 1), 0) & (Q_TILE - 1))
    tri = tri_ref[...]
    qz = []
    for pr in range(SB_PAIRS):
        q2 = q_ref[:, pr * LANES:(pr + 1) * LANES] * SCORE_SCALE
        qz.append(jnp.concatenate([jnp.where(low, q2, 0.0), jnp.where(low, 0.0, q2)], axis=0).astype(BF16))
    acc_ref[...] = jnp.zeros(acc_ref.shape, F32)

    def alive(carry):
        return (carry[0] <= qt) & (carry[1] > 0)

    def chunk(carry):
        i, _, laters = carry
        k0 = pl.multiple_of((qt - i) * SB_CHUNK, SB_CHUNK)
        valid = (k0 + _iota((1, SB_CHUNK), 1)) < tpos
        new = []
        for pr in range(SB_PAIRS):
            sl = slice(pr * LANES, (pr + 1) * LANES)
            z = lax.dot_general(qz[pr], k_ref[pl.ds(k0, SB_CHUNK), sl], NT_DIMS, preferred_element_type=F32)
            a, total = _stick_weights(z, valid, laters[pr], tri)
            acc_ref[pr] += jnp.dot(a.astype(BF16), v_ref[pl.ds(k0, SB_CHUNK), sl], preferred_element_type=F32)
            new.append(laters[pr] + total)
        live = jnp.max(functools.reduce(jnp.maximum, new)) > UNDERFLOW_LOG
        return i + 1, live.astype(jnp.int32), tuple(new)

    init = (jnp.int32(0), jnp.int32(1), tuple(jnp.zeros((2 * Q_TILE, 1), F32) for _ in range(SB_PAIRS)))
    lax.while_loop(alive, chunk, init)
    for pr in range(SB_PAIRS):
        o_ref[:, pr * LANES:(pr + 1) * LANES] = jnp.where(low, acc_ref[pr, :Q_TILE], acc_ref[pr, Q_TILE:])


def sb_attn_prompt(u, kv16, tri, n_batch):
    n = u.shape[0]
    t = n // n_batch
    nt = t // Q_TILE
    return pl.pallas_call(
        _sb_attn_kernel,
        grid=(n_batch, nt),
        in_specs=[pl.BlockSpec((Q_TILE, MIXER_WIDTH), lambda b, i: (b * nt + i, 0)),
                  pl.BlockSpec((t, MIXER_WIDTH), lambda b, i: (b, 0)),
                  pl.BlockSpec((t, MIXER_WIDTH), lambda b, i: (b, 1)),
                  pl.BlockSpec(tri.shape, lambda b, i: (0, 0))],
        out_specs=pl.BlockSpec((Q_TILE, MIXER_WIDTH), lambda b, i: (b * nt + i, 0)),
        out_shape=jax.ShapeDtypeStruct((n, MIXER_WIDTH), F32),
        scratch_shapes=[pltpu.VMEM((SB_PAIRS, 2 * Q_TILE, LANES), F32)],
        compiler_params=_params("parallel", "parallel"),
        name="sb_attn_prompt",
    )(u, kv16, kv16, tri)


def _sb_sample_kernel(pt_ref, q_ref, tri_ref, suffix_ref, *refs):
    page_refs, o_ref = refs[:-1], refs[-1]
    n_pages = len(page_refs)
    head = _iota((SAMPLE_ROWS, 1), 0)
    lane_head = _iota((1, MIXER_WIDTH), 1) >> 6
    qz = (jnp.where(lane_head == head, jnp.broadcast_to(q_ref[0], (SAMPLE_ROWS, MIXER_WIDTH)), 0.0)
          * SCORE_SCALE).astype(BF16)
    z = jnp.concatenate(
        [jnp.dot(qz, page_refs[p][0:MIXER_WIDTH, :].astype(BF16), preferred_element_type=F32) for p in range(n_pages)],
        axis=0)
    sp = _softplus(z)
    log_stay = -sp
    within = _dot_x3(log_stay, tri_ref[...])
    total = jnp.broadcast_to(within[:, 0:1] + log_stay[:, 0:1], z.shape)
    later = _dot_w3(suffix_ref[...], total)
    a = jnp.exp(z - sp + within + later)
    acc = sum(_dot_nt(a[p * SAMPLE_ROWS:(p + 1) * SAMPLE_ROWS], page_refs[p][MIXER_WIDTH:2 * MIXER_WIDTH, :])
              for p in range(n_pages))
    o_ref[0] = jnp.sum(jnp.where(lane_head == head, acc, 0.0), axis=0, keepdims=True)


def sb_attn_sample(page_table, layer, q, pool_t, tri, suffix):
    s, n_pages = page_table.shape
    page_spec = lambda p: pl.BlockSpec((None, None, 2 * MIXER_WIDTH, PAGE_SIZE),
                                       lambda b, pt: (layer, pt[b * n_pages + p], 0, 0))
    grid_spec = pltpu.PrefetchScalarGridSpec(
        num_scalar_prefetch=1,
        grid=(s,),
        in_specs=[pl.BlockSpec((1, 1, MIXER_WIDTH), lambda b, pt: (b, 0, 0)),
                  pl.BlockSpec(tri.shape, lambda b, pt: (0, 0)),
                  pl.BlockSpec(suffix.shape, lambda b, pt: (0, 0))]
                 + [page_spec(p) for p in range(n_pages)],
        out_specs=pl.BlockSpec((1, 1, MIXER_WIDTH), lambda b, pt: (b, 0, 0)),
    )
    return pl.pallas_call(
        _sb_sample_kernel,
        grid_spec=grid_spec,
        out_shape=jax.ShapeDtypeStruct((s, 1, MIXER_WIDTH), F32),
        compiler_params=_params("arbitrary"),
        name="sb_attn_sample",
    )(page_table.reshape(-1), q, tri, suffix, *([pool_t] * n_pages))


def _rope_tables(positions):
    half = ROT_DIM // 2
    inv = np.exp(-math.log(ROPE_THETA) * np.arange(0, ROT_DIM, 2, dtype=np.float64) / ROT_DIM)
    ang = np.asarray(positions, np.float64)[:, None] * inv[None, :]
    lane = np.arange(LANES) % HEAD_DIM
    cos = np.ones((len(positions), LANES))
    sa = np.zeros((len(positions), LANES))
    sb = np.zeros((len(positions), LANES))
    for l in range(LANES):
        r = lane[l]
        if r < half:
            cos[:, l] = np.cos(ang[:, r])
            sb[:, l] = -np.sin(ang[:, r])
        elif r < ROT_DIM:
            cos[:, l] = np.cos(ang[:, r - half])
            sa[:, l] = np.sin(ang[:, r - half])
    return tuple(jnp.asarray(a, F32) for a in (cos, sa, sb))


def _segment_ones():
    lane = np.arange(LANES)
    return jnp.asarray(lane[:, None] // HEAD_DIM == lane[None, :] // HEAD_DIM, BF16)


def _gate_expand():
    e = np.zeros((3, LANES, MIXER_WIDTH), np.float32)
    for j in range(3):
        for h in range(NSA_HEADS):
            e[j, 3 * h + j, h * HEAD_DIM:(h + 1) * HEAD_DIM] = 1.0
    return jnp.asarray(e, BF16)


def _later_ones(n):
    idx = np.arange(n)
    return jnp.asarray(idx[:, None] > idx[None, :], BF16)


def _later_pages(n_pages):
    p = np.repeat(np.arange(n_pages), SAMPLE_ROWS)
    r = np.tile(np.arange(SAMPLE_ROWS), n_pages)
    return jnp.asarray((r[:, None] == r[None, :]) & (p[None, :] > p[:, None]), BF16)


def _compress_weights(w_cmp):
    w = w_cmp.reshape(2, NSA_BLOCK, HEAD_DIM, HEAD_DIM)
    out = jnp.zeros((2, NSA_BLOCK, NSA_GROUPS, HEAD_DIM, NSA_GROUPS, HEAD_DIM), F32)
    for g in range(NSA_GROUPS):
        out = out.at[:, :, g, :, g, :].set(w)
    return out.reshape(2, NSA_BLOCK, LANES, LANES).astype(BF16)


def _compress_weights_t(w_cmp):
    w = w_cmp.reshape(2, NSA_BLOCK, HEAD_DIM, HEAD_DIM).transpose(0, 2, 1, 3)
    halves = PAGE_SIZE // NSA_BLOCK
    out = jnp.zeros((2, HEAD_DIM, halves, NSA_BLOCK, halves, HEAD_DIM), F32)
    for h in range(halves):
        out = out.at[:, :, h, :, h, :].set(w)
    return out.reshape(2, HEAD_DIM, PAGE_SIZE, halves * HEAD_DIM).astype(BF16)


def _rows_last(x):
    a, b, rows = x.shape[:3]
    perm = (0, 1) + tuple(range(3, x.ndim)) + (2,)
    return x.transpose(perm).reshape(a, b, -1, rows)


def kernel(x_prompt, x_sample, cache_nsa_kv, state_nsa_win, state_conv, cache_sb_kv, cache_mem_kv, page_table,
           mem_prompt, norm_mix, norm_ffn, norm_mem, w_in_nsa, w_in_conv, w_in_sb, w_out, w_mem_kv, qk_norm_nsa,
           qk_norm_mem, w_nsa_cmp, conv_w, conv_b, conv_ln_g, conv_ln_b, w_ffn_in, w_ffn_out):
    n_batch, seq, d_model = x_prompt.shape
    n_dec = x_sample.shape[0]
    depth = w_out.shape[0]
    n_pages = page_table.shape[1]
    past = n_pages * PAGE_SIZE
    n_mem = mem_prompt.shape[1]
    ffn_hidden = w_ffn_out.shape[1]
    assert x_sample.shape[1] == 1 and seq % CONV_TILE == 0 and seq >= NSA_WINDOW + Q_TILE
    assert past % NSA_BLOCK == 0 and ffn_hidden % FFN_CHUNK == 0 and n_dec % MEM_BATCH == 0
    assert PAGE_SIZE == SB_CHUNK == LANES and state_nsa_win.shape[2] % LANES == 0

    xp = x_prompt.reshape(n_batch * seq, d_model)
    xs = x_sample.reshape(n_dec, d_model)

    seg = _segment_ones()
    gexp = _gate_expand()
    tri = _later_ones(SB_CHUNK)
    suffix = _later_pages(n_pages)
    rope_p = _rope_tables(np.arange(seq))
    rope_s = _rope_tables(np.full((n_dec,), past))

    nsa_pool_t = _rows_last(cache_nsa_kv)
    nsa_win_t = _rows_last(state_nsa_win)
    sb_pool_t = _rows_last(cache_sb_kv)
    mem_cache_t = _rows_last(cache_mem_kv)
    conv_state_t = state_conv.transpose(0, 2, 1, 3)

    nq, nkv = MIXER_WIDTH, 6 * NSA_GROUPS * HEAD_DIM
    n_gate = 3 * NSA_HEADS
    w_nsa = jnp.concatenate(
        [w_in_nsa[:, :, :nq + nkv], w_in_nsa[:, :, nq + nkv + n_gate:], w_in_nsa[:, :, nq + nkv:nq + nkv + n_gate],
         jnp.zeros(w_in_nsa.shape[:2] + (LANES - n_gate,), F32)], axis=-1).astype(BF16)
    w_conv = w_in_conv.astype(BF16)
    w_sb = w_in_sb.astype(BF16)
    w_o1 = w_out[:, :MIXER_WIDTH].astype(BF16)
    w_o2 = w_out[:, MIXER_WIDTH:].astype(BF16)
    n_ch = ffn_hidden // FFN_CHUNK
    w_gate = w_ffn_in[:, :, :ffn_hidden].reshape(depth, d_model, n_ch, FFN_CHUNK).transpose(0, 2, 1, 3).astype(BF16)
    w_up = w_ffn_in[:, :, ffn_hidden:].reshape(depth, d_model, n_ch, FFN_CHUNK).transpose(0, 2, 1, 3).astype(BF16)
    w_down = w_ffn_out.reshape(depth, n_ch, FFN_CHUNK, d_model).astype(BF16)

    gq_mem = jnp.tile(qk_norm_mem[:, 0], (1, LANES // HEAD_DIM))
    gk_mem = jnp.tile(qk_norm_mem[:, 1], (1, LANES // HEAD_DIM))
    mem_kv = mem_kv_all(mem_prompt.reshape(n_batch * n_mem, d_model), norm_mem, w_mem_kv.astype(BF16), gk_mem, seg)

    nsa_p, nsa_s, win_p, win_s, conv_p, conv_s, sb_p, sb_s = [], [], [], [], [], [], [], []
    for layer in range(depth):
        kind, j = layer % N_MIXERS, layer // N_MIXERS
        if kind == 0:
            up = proj_in(xp, norm_mix[layer], w_nsa[j])
            us = proj_in(xs, norm_mix[layer], w_nsa[j])
            mem_col = 2 * MIXER_WIDTH // MEM_WIDTH
            gq = jnp.tile(qk_norm_nsa[j, 0], LANES // HEAD_DIM).reshape(1, LANES)
            gkv = jnp.tile(qk_norm_nsa[j, 1:4], (1, LANES // HEAD_DIM)).reshape(3, 1, LANES)

            q_p, cs_p, wn_p, gt_p = nsa_prep(up, rope_p, gq, gkv, seg, seq // 256, 256)
            cmp_p = compress(cs_p, _compress_weights(w_nsa_cmp[j]))
            o_p = nsa_attn_prompt(q_p, gt_p, cs_p, wn_p, cmp_p, gexp, n_batch)

            q_s, cs_s, wn_s, gt_s = nsa_prep(us, rope_s, gq, gkv, seg, 1, n_dec)
            cmp_pool = compress_pool(nsa_pool_t, j, _compress_weights_t(w_nsa_cmp[j]))
            o_s, win_new_t = nsa_sample(page_table, j, q_s.reshape(n_dec, 1, MIXER_WIDTH), cmp_pool,
                                        cs_s.reshape(n_dec, 1, 4 * LANES), nsa_win_t,
                                        wn_s.reshape(n_dec, 1, 2 * LANES), gt_s.reshape(n_dec, 1, LANES), gexp,
                                        nsa_pool_t)
            o_s = o_s.reshape(n_dec, MIXER_WIDTH)

            lw = nsa_win_t.shape[-1]
            nsa_p.append(cs_p.reshape(n_batch, seq, 4, NSA_GROUPS, HEAD_DIM))
            nsa_s.append(cs_s.reshape(n_dec, 1, 4, NSA_GROUPS, HEAD_DIM))
            win_p.append(wn_p.reshape(n_batch, seq, 2, NSA_GROUPS, HEAD_DIM)[:, seq - min(NSA_WINDOW, seq):])
            win_s.append(win_new_t.reshape(n_dec, 2, NSA_GROUPS, HEAD_DIM, lw).transpose(0, 4, 1, 2, 3))
        elif kind == 1:
            up = proj_in(xp, norm_mix[layer], w_conv[j])
            us = proj_in(xs, norm_mix[layer], w_conv[j])
            mem_col = 2 * MIXER_WIDTH // MEM_WIDTH
            cw = jnp.concatenate([conv_w[j], jnp.zeros((CONV_HALO - CONV_WIDTH, MIXER_WIDTH), F32)], axis=0)
            vecs = [v[j].reshape(1, MIXER_WIDTH) for v in (conv_b, conv_ln_g, conv_ln_b)]
            o_p, tail = conv_prompt(up, cw, *vecs, n_batch)
            o_s, glu_s = conv_sample(us, conv_state_t, j, cw, *vecs)
            conv_p.append(tail[:, CONV_HALO - (CONV_WIDTH - 1):])
            conv_s.append(jnp.concatenate([conv_state_t[j, 1:], glu_s[None]], axis=0).transpose(1, 0, 2))
        else:
            up, kv16 = proj_in(xp, norm_mix[layer], w_sb[j], narrow_cols=(MIXER_WIDTH, 3 * MIXER_WIDTH))
            us = proj_in(xs, norm_mix[layer], w_sb[j])
            mem_col = 3 * MIXER_WIDTH // MEM_WIDTH
            o_p = sb_attn_prompt(up, kv16, tri, n_batch)
            o_s = sb_attn_sample(page_table, j, us[:, :MIXER_WIDTH].reshape(n_dec, 1, MIXER_WIDTH), sb_pool_t, tri,
                                 suffix).reshape(n_dec, MIXER_WIDTH)
            sb_p.append(up[:, MIXER_WIDTH:3 * MIXER_WIDTH].reshape(n_batch, seq, 2, NSA_HEADS, HEAD_DIM))
            sb_s.append(us[:, MIXER_WIDTH:3 * MIXER_WIDTH].reshape(n_dec, 1, 2, NSA_HEADS, HEAD_DIM))

        gq_l = gq_mem[layer].reshape(1, LANES)
        om_p = mem_attn_prompt(up, mem_col, mem_kv[layer], gq_l, seg, n_batch)
        om_s = mem_attn_sample(us[:, mem_col * MEM_WIDTH:(mem_col + 1) * MEM_WIDTH], mem_cache_t, layer, gq_l, seg)

        ffn_w = (w_o1[layer], w_o2[layer], norm_ffn[layer], w_gate[layer], w_up[layer], w_down[layer])
        xp = out_ffn(o_p, om_p, xp, *ffn_w)
        xs = out_ffn(o_s, om_s, xs, *ffn_w)

    return (xp.reshape(x_prompt.shape), xs.reshape(x_sample.shape),
            jnp.stack(nsa_p), jnp.stack(nsa_s), jnp.stack(win_p), jnp.stack(win_s),
            jnp.stack(conv_p), jnp.stack(conv_s), jnp.stack(sb_p), jnp.stack(sb_s),
            mem_kv.reshape(depth, n_batch, n_mem, 2, MEM_HEADS, HEAD_DIM))
```

```python
import functools
import math

import numpy as np
import jax
import jax.numpy as jnp
from jax import lax
from jax.experimental import pallas as pl
from jax.experimental.pallas import tpu as pltpu

F32 = jnp.float32
BF16 = jnp.bfloat16

HEAD_DIM = 64
MEM_HEADS = 4
MEM_WIDTH = MEM_HEADS * HEAD_DIM
N_MIXERS = 3
NSA_HEADS = 12
NSA_GROUPS = 2
NSA_HPG = NSA_HEADS // NSA_GROUPS
NSA_BLOCK = 64
NSA_TOPK = 16
NSA_WINDOW = 512
MIXER_WIDTH = NSA_HEADS * HEAD_DIM
CONV_WIDTH = 31
ROPE_THETA = 500000.0
ROT_DIM = HEAD_DIM // 4
EPS = 1e-6
PAGE_SIZE = 128

LANES = 128
SUBLANES = 8
VMEM_LIMIT = 56 * 1024 * 1024

ROW_TILE = 512
FFN_TILE = 256
Q_TILE = 128
SEL_CHUNK = 512
SB_CHUNK = 128
CONV_TILE = 512
CMP_BLOCKS = 256
CMP_PAGES = 128
MEM_BATCH = 8
CONV_HALO = 32

SCORE_SCALE = HEAD_DIM ** -0.5
BIG = 1e30
UNDERFLOW_LOG = -104.0
NT_DIMS = (((1,), (1,)), ((), ()))


def _params(*sem):
    return pltpu.CompilerParams(dimension_semantics=sem, vmem_limit_bytes=VMEM_LIMIT)


def _pick_tile(n, cap):
    return max(t for t in range(SUBLANES, cap + 1, SUBLANES) if n % t == 0)


def _iota(shape, dim):
    return lax.broadcasted_iota(jnp.int32, shape, dim)


def _dot(a, b):
    return jnp.dot(a.astype(BF16), b.astype(BF16), preferred_element_type=F32)


def _dot_nt(a, b):
    return lax.dot_general(a.astype(BF16), b.astype(BF16), NT_DIMS, preferred_element_type=F32)


def _split2(x):
    hi = x.astype(BF16)
    lo = (x - hi.astype(F32)).astype(BF16)
    return hi, lo


def _split3(x):
    hi = x.astype(BF16)
    r = x - hi.astype(F32)
    mid = r.astype(BF16)
    lo = (r - mid.astype(F32)).astype(BF16)
    return hi, mid, lo


def _dot_x2(x, w_bf16):
    return sum(jnp.dot(part, w_bf16, preferred_element_type=F32) for part in _split2(x))


def _dot_x3(x, w_bf16):
    return sum(jnp.dot(part, w_bf16, preferred_element_type=F32) for part in _split3(x))


def _dot_w3(w_bf16, x):
    return sum(jnp.dot(w_bf16, part, preferred_element_type=F32) for part in _split3(x))


def _dot_nt_precise(a, b):
    ah, al = _split2(a)
    bh, bl = _split2(b)
    return (lax.dot_general(ah, bh, NT_DIMS, preferred_element_type=F32)
            + lax.dot_general(ah, bl, NT_DIMS, preferred_element_type=F32)
            + lax.dot_general(al, bh, NT_DIMS, preferred_element_type=F32))


def _bf16_round(x):
    return x.astype(BF16).astype(F32)


def _rms_rows(x, g):
    ms = jnp.mean(x * x, axis=-1, keepdims=True)
    return x * lax.rsqrt(ms + EPS) * g


def _head_rms(x, seg_ones, g):
    ms = _dot_x2(x * x, seg_ones) * (1.0 / HEAD_DIM)
    return x * lax.rsqrt(ms + EPS) * g


def _sigmoid(x):
    return 1.0 / (1.0 + jnp.exp(-x))


def _softplus(z):
    return jnp.maximum(z, 0.0) + jnp.log1p(jnp.exp(-jnp.abs(z)))


def _proj_in_kernel(x_ref, g_ref, w_ref, o_ref, *narrow_ref, narrow_cols):
    h = _rms_rows(x_ref[...], g_ref[...]).astype(BF16)
    u = jnp.dot(h, w_ref[...], preferred_element_type=F32)
    o_ref[...] = u
    if narrow_cols is not None:
        narrow_ref[0][...] = u[:, narrow_cols[0]:narrow_cols[1]].astype(BF16)


def proj_in(x, g, w, narrow_cols=None):
    n, d = x.shape
    m = w.shape[1]
    tm = min(ROW_TILE, n)
    out_specs = [pl.BlockSpec((tm, m), lambda i: (i, 0))]
    out_shape = [jax.ShapeDtypeStruct((n, m), F32)]
    if narrow_cols is not None:
        width = narrow_cols[1] - narrow_cols[0]
        out_specs.append(pl.BlockSpec((tm, width), lambda i: (i, 0)))
        out_shape.append(jax.ShapeDtypeStruct((n, width), BF16))
    res = pl.pallas_call(
        functools.partial(_proj_in_kernel, narrow_cols=narrow_cols),
        grid=(n // tm,),
        in_specs=[pl.BlockSpec((tm, d), lambda i: (i, 0)),
                  pl.BlockSpec((1, d), lambda i: (0, 0)),
                  pl.BlockSpec((d, m), lambda i: (0, 0))],
        out_specs=out_specs,
        out_shape=out_shape,
        compiler_params=_params("parallel"),
        name="proj_in",
    )(x, g.reshape(1, d), w)
    return res if narrow_cols is not None else res[0]


def _out_ffn_kernel(o_ref, om_ref, x_ref, w1_ref, w2_ref, gf_ref, wgu_ref, wo_ref, y_ref):
    x = x_ref[...] + _dot(o_ref[...], w1_ref[...]) + _dot(om_ref[...], w2_ref[...])
    h = _rms_rows(x, gf_ref[...]).astype(BF16)
    hidden = wo_ref.shape[0]
    gate_up = jnp.dot(h, wgu_ref[...], preferred_element_type=F32)
    gate, up = gate_up[:, :hidden], gate_up[:, hidden:]
    act = (gate * _sigmoid(gate) * up).astype(BF16)
    y_ref[...] = x + jnp.dot(act, wo_ref[...], preferred_element_type=F32)


def out_ffn(o, om, x, w1, w2, gf, wgu, wo):
    n, d = x.shape
    tm = min(FFN_TILE, n)
    const = lambda *shape: pl.BlockSpec(shape, lambda i: (0,) * len(shape), pipeline_mode=pl.Buffered(1))
    return pl.pallas_call(
        _out_ffn_kernel,
        grid=(n // tm,),
        in_specs=[pl.BlockSpec((tm, o.shape[1]), lambda i: (i, 0)),
                  pl.BlockSpec((tm, om.shape[1]), lambda i: (i, 0)),
                  pl.BlockSpec((tm, d), lambda i: (i, 0)),
                  const(*w1.shape), const(*w2.shape), const(1, d), const(*wgu.shape), const(*wo.shape)],
        out_specs=pl.BlockSpec((tm, d), lambda i: (i, 0)),
        out_shape=jax.ShapeDtypeStruct((n, d), F32),
        compiler_params=_params("parallel"),
        name="out_ffn",
    )(o, om, x, w1, w2, gf.reshape(1, d), wgu, wo)


def _mem_kv_kernel(mem_ref, gn_ref, w_ref, gk_ref, seg_ref, o_ref):
    h = _rms_rows(mem_ref[...], gn_ref[0]).astype(BF16)
    kv = jnp.dot(h, w_ref[0], preferred_element_type=F32)
    seg = seg_ref[...]
    for c in range(MEM_WIDTH // LANES):
        sl = slice(c * LANES, (c + 1) * LANES)
        o_ref[0, :, sl] = _head_rms(kv[:, sl], seg, gk_ref[0])
    o_ref[0, :, MEM_WIDTH:] = kv[:, MEM_WIDTH:]


def mem_kv_all(mem, g_norm, w_kv, g_k, seg):
    n, d = mem.shape
    nl = w_kv.shape[0]
    return pl.pallas_call(
        _mem_kv_kernel,
        grid=(nl,),
        in_specs=[pl.BlockSpec((n, d), lambda l: (0, 0)),
                  pl.BlockSpec((1, 1, d), lambda l: (l, 0, 0)),
                  pl.BlockSpec((1, d, 2 * MEM_WIDTH), lambda l: (l, 0, 0)),
                  pl.BlockSpec((1, 1, LANES), lambda l: (l, 0, 0)),
                  pl.BlockSpec((LANES, LANES), lambda l: (0, 0))],
        out_specs=pl.BlockSpec((1, n, 2 * MEM_WIDTH), lambda l: (l, 0, 0)),
        out_shape=jax.ShapeDtypeStruct((nl, n, 2 * MEM_WIDTH), F32),
        compiler_params=_params("arbitrary"),
        name="mem_kv",
    )(mem, g_norm.reshape(nl, 1, d), w_kv, g_k.reshape(nl, 1, LANES), seg)


def _mem_query(qm, seg, gq):
    parts = [_head_rms(qm[:, c * LANES:(c + 1) * LANES], seg, gq) for c in range(MEM_WIDTH // LANES)]
    return jnp.concatenate(parts, axis=1) * SCORE_SCALE


def _softmax_rows(s):
    p = jnp.exp(s - jnp.max(s, axis=-1, keepdims=True))
    return p / jnp.sum(p, axis=-1, keepdims=True)


def _mem_attn_kernel(qm_ref, kv_ref, gq_ref, seg_ref, o_ref):
    qn = _mem_query(qm_ref[...], seg_ref[...], gq_ref[...])
    r = qn.shape[0]
    head_of_lane = _iota((1, MEM_WIDTH), 1) >> 6
    qz = jnp.concatenate([jnp.where(head_of_lane == h, qn, 0.0) for h in range(MEM_HEADS)], axis=0)
    p = _softmax_rows(_dot_nt(qz, kv_ref[:, :MEM_WIDTH]))
    res = _dot(p, kv_ref[:, MEM_WIDTH:])
    out = jnp.zeros((r, MEM_WIDTH), F32)
    for h in range(MEM_HEADS):
        out = out + jnp.where(head_of_lane == h, res[h * r:(h + 1) * r], 0.0)
    o_ref[...] = out


def mem_attn_prompt(u, mem_col, kv, gq, seg, n_batch):
    n = u.shape[0]
    t = n // n_batch
    tm = 256
    n_mem = kv.shape[0] // n_batch
    nt = t // tm
    return pl.pallas_call(
        _mem_attn_kernel,
        grid=(n_batch, nt),
        in_specs=[pl.BlockSpec((tm, MEM_WIDTH), lambda b, i: (b * nt + i, mem_col)),
                  pl.BlockSpec((n_mem, 2 * MEM_WIDTH), lambda b, i: (b, 0)),
                  pl.BlockSpec((1, LANES), lambda b, i: (0, 0)),
                  pl.BlockSpec((LANES, LANES), lambda b, i: (0, 0))],
        out_specs=pl.BlockSpec((tm, MEM_WIDTH), lambda b, i: (b * nt + i, 0)),
        out_shape=jax.ShapeDtypeStruct((n, MEM_WIDTH), F32),
        compiler_params=_params("parallel", "parallel"),
        name="mem_attn_prompt",
    )(u, kv, gq, seg)


def _mem_attn_sample_kernel(qm_ref, kv_ref, gq_ref, seg_ref, o_ref):
    qn = _mem_query(qm_ref[0], seg_ref[...], gq_ref[...])
    row = _iota((SUBLANES, 1), 0)
    head_of_lane = _iota((1, MEM_WIDTH), 1) >> 6
    out = jnp.zeros((MEM_BATCH, MEM_WIDTH), F32)
    for i in range(MEM_BATCH):
        qz = jnp.where(head_of_lane == (row & (MEM_HEADS - 1)), jnp.broadcast_to(qn[i:i + 1], (SUBLANES, MEM_WIDTH)), 0.0)
        p = _softmax_rows(_dot(qz, kv_ref[i, 0:MEM_WIDTH, :]))
        res = _dot_nt(p, kv_ref[i, MEM_WIDTH:2 * MEM_WIDTH, :])
        flat = jnp.sum(jnp.where(head_of_lane == row, res, 0.0), axis=0, keepdims=True)
        out = out + jnp.where(_iota((MEM_BATCH, 1), 0) == i, flat, 0.0)
    o_ref[0] = out


def mem_attn_sample(qm, kv_t, layer, gq, seg):
    _, s, _, n_mem = kv_t.shape
    nb = s // MEM_BATCH
    return pl.pallas_call(
        _mem_attn_sample_kernel,
        grid=(nb,),
        in_specs=[pl.BlockSpec((1, MEM_BATCH, MEM_WIDTH), lambda b: (b, 0, 0)),
                  pl.BlockSpec((None, MEM_BATCH, 2 * MEM_WIDTH, n_mem), lambda b: (layer, b, 0, 0)),
                  pl.BlockSpec((1, LANES), lambda b: (0, 0)),
                  pl.BlockSpec((LANES, LANES), lambda b: (0, 0))],
        out_specs=pl.BlockSpec((1, MEM_BATCH, MEM_WIDTH), lambda b: (b, 0, 0)),
        out_shape=jax.ShapeDtypeStruct((nb, MEM_BATCH, MEM_WIDTH), F32),
        compiler_params=_params("parallel"),
        name="mem_attn_sample",
    )(qm.reshape(nb, MEM_BATCH, MEM_WIDTH), kv_t, gq, seg).reshape(s, MEM_WIDTH)


def _nsa_prep_kernel(q_ref, kv_ref, gt_ref, cos_ref, sa_ref, sb_ref, gq_ref, gkv_ref, seg_ref,
                     qo_ref, cso_ref, wino_ref, go_ref):
    cos, sa, sb = cos_ref[...], sa_ref[...], sb_ref[...]
    seg = seg_ref[...]
    half = ROT_DIM // 2

    def norm_rope(x, g):
        y = _head_rms(x, seg, g)
        return y * cos + pltpu.roll(y, half, 1) * sa + pltpu.roll(y, LANES - half, 1) * sb

    for c in range(MIXER_WIDTH // LANES):
        sl = slice(c * LANES, (c + 1) * LANES)
        qo_ref[:, sl] = norm_rope(q_ref[:, sl], gq_ref[...])
    for c in range(6):
        x = kv_ref[:, c * LANES:(c + 1) * LANES]
        if c % 2 == 0:
            x = norm_rope(x, gkv_ref[c // 2])
        if c < 4:
            cso_ref[:, c * LANES:(c + 1) * LANES] = x
        else:
            wino_ref[:, (c - 4) * LANES:(c - 3) * LANES] = x
    go_ref[...] = _sigmoid(gt_ref[...])


def nsa_prep(u, rope_tabs, gq, gkv, seg, n_time_tiles, tm):
    n = u.shape[0]
    cos, sa, sb = rope_tabs
    tab = pl.BlockSpec((tm, LANES), lambda i: (i % n_time_tiles, 0))
    const = lambda *shape: pl.BlockSpec(shape, lambda i: (0,) * len(shape))
    return pl.pallas_call(
        _nsa_prep_kernel,
        grid=(n // tm,),
        in_specs=[pl.BlockSpec((tm, MIXER_WIDTH), lambda i: (i, 0)),
                  pl.BlockSpec((tm, MIXER_WIDTH), lambda i: (i, 1)),
                  pl.BlockSpec((tm, LANES), lambda i: (i, (2 * MIXER_WIDTH + MEM_WIDTH) // LANES)),
                  tab, tab, tab, const(1, LANES), const(3, 1, LANES), const(LANES, LANES)],
        out_specs=[pl.BlockSpec((tm, MIXER_WIDTH), lambda i: (i, 0)),
                   pl.BlockSpec((tm, 4 * LANES), lambda i: (i, 0)),
                   pl.BlockSpec((tm, 2 * LANES), lambda i: (i, 0)),
                   pl.BlockSpec((tm, LANES), lambda i: (i, 0))],
        out_shape=[jax.ShapeDtypeStruct((n, MIXER_WIDTH), F32),
                   jax.ShapeDtypeStruct((n, 4 * LANES), F32),
                   jax.ShapeDtypeStruct((n, 2 * LANES), F32),
                   jax.ShapeDtypeStruct((n, LANES), F32)],
        compiler_params=_params("parallel"),
        name="nsa_prep",
    )(u, u, u, cos, sa, sb, gq, gkv, seg)


def _compress_kernel(xk_ref, xv_ref, w_ref, o_ref):
    mb = o_ref.shape[0]
    for c, x_ref in enumerate((xk_ref, xv_ref)):
        acc = jnp.zeros((mb, LANES), F32)
        for r in range(NSA_BLOCK):
            xr = x_ref[pl.ds(r, mb, stride=NSA_BLOCK), :]
            acc = acc + jnp.dot(xr.astype(BF16), w_ref[c, r], preferred_element_type=F32)
        o_ref[:, c * LANES:(c + 1) * LANES] = acc


def compress(cs, wbd):
    nblk = cs.shape[0] // NSA_BLOCK
    mb = _pick_tile(nblk, CMP_BLOCKS)
    return pl.pallas_call(
        _compress_kernel,
        grid=(nblk // mb,),
        in_specs=[pl.BlockSpec((mb * NSA_BLOCK, LANES), lambda i: (i, 0)),
                  pl.BlockSpec((mb * NSA_BLOCK, LANES), lambda i: (i, 1)),
                  pl.BlockSpec(wbd.shape, lambda i: (0, 0, 0, 0))],
        out_specs=pl.BlockSpec((mb, 2 * LANES), lambda i: (i, 0)),
        out_shape=jax.ShapeDtypeStruct((nblk, 2 * LANES), F32),
        compiler_params=_params("parallel"),
        name="nsa_compress",
    )(cs, cs, wbd)


def _compress_pool_kernel(x_ref, w_ref, o_ref):
    mbp = o_ref.shape[0]
    low = _iota((1, LANES), 1) < HEAD_DIM
    for c in range(2):
        slabs = []
        for g in range(NSA_GROUPS):
            acc = jnp.zeros((mbp, LANES), F32)
            for d in range(HEAD_DIM):
                xr = x_ref[:, c * LANES + g * HEAD_DIM + d, :]
                acc = acc + jnp.dot(xr.astype(BF16), w_ref[c, d], preferred_element_type=F32)
            slabs.append(acc)
        first = jnp.where(low, slabs[0], pltpu.roll(slabs[1], HEAD_DIM, 1))
        second = jnp.where(low, pltpu.roll(slabs[0], HEAD_DIM, 1), slabs[1])
        o_ref[:, c * LANES:(c + 1) * LANES] = first
        o_ref[:, (2 + c) * LANES:(3 + c) * LANES] = second


def compress_pool(pool_t, layer, wt):
    n_pool = pool_t.shape[1]
    mbp = _pick_tile(n_pool, CMP_PAGES)
    return pl.pallas_call(
        _compress_pool_kernel,
        grid=(n_pool // mbp,),
        in_specs=[pl.BlockSpec((None, mbp, 2 * LANES, PAGE_SIZE), lambda i: (layer, i, 0, 0)),
                  pl.BlockSpec(wt.shape, lambda i: (0, 0, 0, 0))],
        out_specs=pl.BlockSpec((mbp, 4 * LANES), lambda i: (i, 0)),
        out_shape=jax.ShapeDtypeStruct((n_pool, 4 * LANES), F32),
        compiler_params=_params("parallel"),
        name="nsa_compress_pool",
    )(pool_t, wt)


def _group_queries(q_chunks, g):
    parts = []
    for h in range(NSA_HPG):
        hh = g * NSA_HPG + h
        chunk = q_chunks(hh // 2)
        if hh % 2 != g:
            chunk = pltpu.roll(chunk, HEAD_DIM, 1)
        parts.append(chunk)
    qz = jnp.concatenate(parts, axis=0) * SCORE_SCALE
    return jnp.where((_iota((1, LANES), 1) >> 6) == g, qz, 0.0)


def _select_blocks(imp, cur):
    r, nb = imp.shape
    nidx = _iota((1, nb), 1)
    score = jnp.where(nidx == cur, BIG, jnp.where(nidx < cur, imp, -1.0))
    rank = jnp.zeros((r, nb), F32)
    for m in range(nb):
        col = score[:, m:m + 1]
        rank = rank + jnp.where(nidx > m, jnp.where(col >= score, 1.0, 0.0), jnp.where(col > score, 1.0, 0.0))
    return (rank < float(NSA_TOPK)) & (nidx <= cur)


def _select_blocks_t(imp, cur, live_blocks, score_ref):
    nb, r = imp.shape
    nidx = _iota((nb, 1), 0)
    score = jnp.where(nidx == cur, BIG, jnp.where(nidx < cur, imp, -1.0))
    score_ref[...] = score

    def count_ahead(m, rank):
        row = score_ref[pl.ds(m, 1), :]
        return rank + jnp.where(nidx > m, jnp.where(row >= score, 1.0, 0.0), jnp.where(row > score, 1.0, 0.0))

    rank = lax.fori_loop(0, live_blocks, count_ahead, jnp.zeros((nb, r), F32))
    return jnp.where((rank < float(NSA_TOPK)) & (nidx <= cur), 1.0, 0.0)


def _masked_softmax_parts(s, ok):
    m = jnp.max(jnp.where(ok, s, -BIG), axis=-1, keepdims=True)
    p = jnp.where(ok, jnp.exp(s - m), 0.0)
    return p, jnp.maximum(jnp.sum(p, axis=-1, keepdims=True), 1e-30)


def _place_heads(res, g, rows):
    low = _iota((1, LANES), 1) < HEAD_DIM
    out = []
    for c in range(NSA_HPG // 2):
        a = res[(2 * c) * rows:(2 * c + 1) * rows]
        b = res[(2 * c + 1) * rows:(2 * c + 2) * rows]
        if g == 1:
            a = pltpu.roll(a, HEAD_DIM, 1)
        else:
            b = pltpu.roll(b, HEAD_DIM, 1)
        out.append(jnp.where(low, a, b))
    return out


def _nsa_attn_kernel(q_ref, g_ref, cs_ref, win_ref, cmp_ref, cmpt_ref, gexp_ref, o_ref, score_ref):
    qt = pl.program_id(1)
    t0 = qt * Q_TILE
    seq = cs_ref.shape[0]
    nb = cmp_ref.shape[0]
    tpos = t0 + _iota((Q_TILE, 1), 0)
    rows = NSA_HPG * Q_TILE

    gates = g_ref[...]
    gexp = [_dot_x2(gates, gexp_ref[j]) for j in range(3)]

    win_span = min(NSA_WINDOW + Q_TILE, seq)
    w0 = pl.multiple_of(jnp.maximum(t0 + Q_TILE - win_span, 0), Q_TILE)
    n_chunks = (t0 + Q_TILE + SEL_CHUNK - 1) // SEL_CHUNK

    t_lane = t0 + (_iota((1, rows), 1) & (Q_TILE - 1))
    block_row = _iota((nb, 1), 0)
    complete = (block_row + 1) * NSA_BLOCK <= t_lane + 1
    cur_lane = (t0 + _iota((1, Q_TILE), 1)) >> 6
    live_blocks = jnp.minimum((t0 + Q_TILE - 1) // NSA_BLOCK + 1, nb)

    branch_out = []
    for g in range(NSA_GROUPS):
        qz = _group_queries(lambda c: q_ref[:, c * LANES:(c + 1) * LANES], g)
        qzb = qz.astype(BF16)

        s = _dot_nt_precise(cmp_ref[:, 0:LANES], qz)
        m_cmp = jnp.max(jnp.where(complete, s, -BIG), axis=0, keepdims=True)
        p = jnp.where(complete, jnp.exp(s - m_cmp), 0.0)
        p_cmp = p / jnp.maximum(jnp.sum(p, axis=0, keepdims=True), 1e-30)
        o_cmp_t = _dot(cmpt_ref[LANES:2 * LANES, :], p_cmp)
        o_cmp = jnp.concatenate([o_cmp_t[:, h * Q_TILE:(h + 1) * Q_TILE].T for h in range(NSA_HPG)], axis=0)
        importance = sum(p_cmp[:, h * Q_TILE:(h + 1) * Q_TILE] for h in range(NSA_HPG))
        sel_t = _select_blocks_t(importance, cur_lane, live_blocks, score_ref)
        sel = jnp.concatenate([sel_t, jnp.zeros((LANES - nb, Q_TILE), F32)], axis=0).T
        sel_bf = sel.astype(BF16)

        def sel_chunk(kc, carry):
            m, l, acc = carry
            k0 = pl.multiple_of(kc * SEL_CHUNK, SEL_CHUNK)
            kch = cs_ref[pl.ds(k0, SEL_CHUNK), 2 * LANES:3 * LANES]
            vch = cs_ref[pl.ds(k0, SEL_CHUNK), 3 * LANES:4 * LANES]
            kpos = k0 + _iota((1, SEL_CHUNK), 1)
            expand = jnp.where((kpos >> 6) == _iota((LANES, 1), 0), 1.0, 0.0).astype(BF16)
            chosen = jnp.dot(sel_bf, expand, preferred_element_type=F32)
            bias = jnp.where((chosen > 0.5) & (kpos <= tpos), 0.0, -BIG)[None]
            sc = lax.dot_general(qzb, kch.astype(BF16), NT_DIMS, preferred_element_type=F32
                                 ).reshape(NSA_HPG, Q_TILE, SEL_CHUNK) + bias
            m_new = jnp.maximum(m, jnp.max(sc, axis=-1, keepdims=True))
            alpha = jnp.exp(m - m_new)
            pp = jnp.exp(sc - m_new)
            l = alpha * l + jnp.sum(pp, axis=-1, keepdims=True)
            pv = _dot(pp.reshape(rows, SEL_CHUNK), vch).reshape(NSA_HPG, Q_TILE, LANES)
            return m_new, l, alpha * acc + pv

        init = (jnp.full((NSA_HPG, Q_TILE, 1), -BIG, F32), jnp.zeros((NSA_HPG, Q_TILE, 1), F32),
                jnp.zeros((NSA_HPG, Q_TILE, LANES), F32))
        _, l, acc = lax.fori_loop(0, n_chunks, sel_chunk, init)
        o_slc = (acc / l).reshape(rows, LANES)

        kw = win_ref[pl.ds(w0, win_span), 0:LANES]
        vw = win_ref[pl.ds(w0, win_span), LANES:2 * LANES]
        wpos = w0 + _iota((1, win_span), 1)
        w_bias = jnp.where((wpos <= tpos) & (wpos > tpos - NSA_WINDOW), 0.0, -BIG)[None]
        sw = lax.dot_general(qzb, kw.astype(BF16), NT_DIMS, preferred_element_type=F32
                             ).reshape(NSA_HPG, Q_TILE, win_span) + w_bias
        pw = jnp.exp(sw - jnp.max(sw, axis=-1, keepdims=True))
        o_win = (_dot(pw.reshape(rows, win_span), vw).reshape(NSA_HPG, Q_TILE, LANES)
                 / jnp.sum(pw, axis=-1, keepdims=True)).reshape(rows, LANES)

        branch_out.append([_place_heads(o, g, Q_TILE) for o in (o_cmp, o_slc, o_win)])

    for c in range(MIXER_WIDTH // LANES):
        g, cc = divmod(c, NSA_HPG // 2)
        sl = slice(c * LANES, (c + 1) * LANES)
        o_ref[:, sl] = sum(gexp[j][:, sl] * branch_out[g][j][cc] for j in range(3))


def nsa_attn_prompt(q, gates, cs, win, cmp, gexp, n_batch):
    n = q.shape[0]
    t = n // n_batch
    nt = t // Q_TILE
    nb = cmp.shape[0] // n_batch
    assert nb <= LANES
    cmp_t = cmp.reshape(n_batch, nb, 2 * LANES).transpose(0, 2, 1)
    return pl.pallas_call(
        _nsa_attn_kernel,
        grid=(n_batch, nt),
        in_specs=[pl.BlockSpec((Q_TILE, MIXER_WIDTH), lambda b, i: (b * nt + i, 0)),
                  pl.BlockSpec((Q_TILE, LANES), lambda b, i: (b * nt + i, 0)),
                  pl.BlockSpec((t, 4 * LANES), lambda b, i: (b, 0)),
                  pl.BlockSpec((t, 2 * LANES), lambda b, i: (b, 0)),
                  pl.BlockSpec((nb, 2 * LANES), lambda b, i: (b, 0)),
                  pl.BlockSpec((None, 2 * LANES, nb), lambda b, i: (b, 0, 0)),
                  pl.BlockSpec(gexp.shape, lambda b, i: (0, 0, 0))],
        out_specs=pl.BlockSpec((Q_TILE, MIXER_WIDTH), lambda b, i: (b * nt + i, 0)),
        out_shape=jax.ShapeDtypeStruct((n, MIXER_WIDTH), F32),
        scratch_shapes=[pltpu.VMEM((nb, Q_TILE), F32)],
        compiler_params=_params("parallel", "parallel"),
        name="nsa_attn_prompt",
    )(q, gates, cs, win, cmp, cmp_t, gexp)


SAMPLE_ROWS = 16
NB_PAD = 64


def _sample_queries(q_row):
    row = _iota((SAMPLE_ROWS, 1), 0)
    lane_group = _iota((1, LANES), 1) >> 6
    qz = jnp.zeros((SAMPLE_ROWS, LANES), F32)
    for hh in range(NSA_HEADS):
        g = hh // NSA_HPG
        chunk = jnp.broadcast_to(q_row[:, (hh // 2) * LANES:(hh // 2 + 1) * LANES], (SAMPLE_ROWS, LANES))
        if hh % 2 != g:
            chunk = pltpu.roll(chunk, HEAD_DIM, 1)
        qz = qz + jnp.where((row == hh) & (lane_group == g), chunk, 0.0)
    return qz * SCORE_SCALE


def _row_to_column(row):
    n = row.shape[1]
    diag = _iota((n, 1), 0) == _iota((1, n), 1)
    return jnp.sum(jnp.where(diag, jnp.broadcast_to(row, (n, n)), 0.0), axis=1, keepdims=True)


def _attend_with_new_key(s, ok, qz, k_new, v_new, weighted_values):
    s_new = jnp.sum(_bf16_round(qz) * _bf16_round(k_new), axis=-1, keepdims=True)
    m_lanes = jnp.max(jnp.where(ok, s, -BIG), axis=-1, keepdims=True)
    if s.ndim == 3:
        m_lanes = jnp.max(m_lanes, axis=0)
    m = jnp.maximum(m_lanes, s_new)
    p = jnp.where(ok, jnp.exp(s - m), 0.0)
    p_new = jnp.exp(s_new - m)
    total = jnp.sum(p, axis=-1, keepdims=True)
    if s.ndim == 3:
        total = jnp.sum(total, axis=0)
    den = total + p_new
    return (weighted_values(p) + _bf16_round(p_new) * _bf16_round(v_new)) / den


def _nsa_sample_kernel(pt_ref, q_ref, cmp_ref, new_ref, win_ref, wnew_ref, g_ref, gexp_ref, *refs, past):
    n_pages = past // PAGE_SIZE
    page_refs = refs[:n_pages]
    o_ref, wout_ref, cm_ref = refs[n_pages:]
    b = pl.program_id(0)
    head = _iota((SAMPLE_ROWS, 1), 0)
    qz = _sample_queries(q_ref[0])
    qzb = qz.astype(BF16)

    cm_ref[...] = jnp.zeros(cm_ref.shape, F32)
    for p in range(n_pages):
        row = cmp_ref[pl.ds(pt_ref[b * n_pages + p], 1), :]
        cm_ref[2 * p:2 * p + 1, :] = row[:, 0:2 * LANES]
        cm_ref[2 * p + 1:2 * p + 2, :] = row[:, 2 * LANES:4 * LANES]
    s = _dot_nt_precise(qz, cm_ref[:, 0:LANES])
    nidx = _iota((1, NB_PAD), 1)
    complete = (nidx + 1) * NSA_BLOCK <= past + 1
    p, den = _masked_softmax_parts(s, complete)
    p_cmp = p / den
    o_cmp = _dot(p_cmp, cm_ref[:, LANES:2 * LANES])

    row8 = _iota((SUBLANES, 1), 0)
    imp = jnp.zeros((SUBLANES, NB_PAD), F32)
    for g in range(NSA_GROUPS):
        in_group = (head >= g * NSA_HPG) & (head < (g + 1) * NSA_HPG)
        imp = imp + jnp.where(row8 == g, jnp.sum(jnp.where(in_group, p_cmp, 0.0), axis=0, keepdims=True), 0.0)
    cur = jnp.full((SUBLANES, 1), past // NSA_BLOCK, jnp.int32)
    sel = jnp.where(_select_blocks(imp, cur), 1.0, 0.0)
    sel_rows = jnp.where(head < NSA_HPG, sel[0:1], sel[1:2]).astype(BF16)
    kpos = _iota((1, past), 1)
    expand = jnp.where((kpos >> 6) == _iota((NB_PAD, 1), 0), 1.0, 0.0).astype(BF16)
    chosen = jnp.dot(sel_rows, expand, preferred_element_type=F32)

    s3 = jnp.concatenate(
        [jnp.dot(qzb, page_refs[p][0:LANES, :].astype(BF16), preferred_element_type=F32) for p in range(n_pages)],
        axis=0).reshape(n_pages, SAMPLE_ROWS, PAGE_SIZE)
    ok3 = jnp.concatenate([chosen[:, p * PAGE_SIZE:(p + 1) * PAGE_SIZE] for p in range(n_pages)],
                          axis=0).reshape(n_pages, SAMPLE_ROWS, PAGE_SIZE) > 0.5

    def sel_values(pp):
        return sum(_dot_nt(pp[p], page_refs[p][LANES:2 * LANES, :]) for p in range(n_pages))

    o_slc = _attend_with_new_key(s3, ok3, qz, new_ref[0][:, 2 * LANES:3 * LANES], new_ref[0][:, 3 * LANES:4 * LANES],
                                 sel_values)

    lw = win_ref.shape[1]
    sw = jnp.dot(qzb, win_ref[0:LANES, :].astype(BF16), preferred_element_type=F32)
    okw = _iota((1, lw), 1) > lw - NSA_WINDOW
    o_win = _attend_with_new_key(sw, okw, qz, wnew_ref[0][:, 0:LANES], wnew_ref[0][:, LANES:2 * LANES],
                                 lambda pp: _dot_nt(pp, win_ref[LANES:2 * LANES, :]))

    gates = jnp.broadcast_to(g_ref[0], (SUBLANES, LANES))
    lane_head = _iota((1, MIXER_WIDTH), 1) >> 6
    in_place = (head & 1) == jnp.where(head >= NSA_HPG, 1, 0)
    out = jnp.zeros((1, MIXER_WIDTH), F32)
    for j, o in enumerate((o_cmp, o_slc, o_win)):
        tiled = jnp.concatenate([o] * (MIXER_WIDTH // LANES), axis=1)
        rolled = jnp.concatenate([pltpu.roll(o, HEAD_DIM, 1)] * (MIXER_WIDTH // LANES), axis=1)
        placed = jnp.where(lane_head == head, jnp.where(in_place, tiled, rolled), 0.0)
        out = out + _dot_x2(gates, gexp_ref[j])[0:1] * jnp.sum(placed, axis=0, keepdims=True)
    o_ref[0] = out

    new_col = jnp.broadcast_to(_row_to_column(wnew_ref[0]), (2 * LANES, LANES))
    last_lane = _iota((1, LANES), 1) == LANES - 1
    for c in range(lw // LANES):
        shifted = pltpu.roll(win_ref[:, c * LANES:(c + 1) * LANES], LANES - 1, 1)
        if (c + 1) * LANES < lw:
            carry_in = pltpu.roll(win_ref[:, (c + 1) * LANES:(c + 2) * LANES], LANES - 1, 1)
        else:
            carry_in = new_col
        wout_ref[:, c * LANES:(c + 1) * LANES] = jnp.where(last_lane, carry_in, shifted)


def nsa_sample(page_table, layer, q, cmp_pool, cs_new, win_t, win_new, gates, gexp, pool_t):
    s, n_pages = page_table.shape
    past = n_pages * PAGE_SIZE
    lw = win_t.shape[-1]
    per_seq = lambda *shape: pl.BlockSpec((1,) + shape, lambda b, pt: (b,) + (0,) * len(shape))
    page_spec = lambda p: pl.BlockSpec((None, None, 2 * LANES, PAGE_SIZE),
                                       lambda b, pt: (layer, pt[b * n_pages + p], 1, 0))
    grid_spec = pltpu.PrefetchScalarGridSpec(
        num_scalar_prefetch=1,
        grid=(s,),
        in_specs=[per_seq(1, MIXER_WIDTH),
                  pl.BlockSpec(cmp_pool.shape, lambda b, pt: (0, 0)),
                  per_seq(1, 4 * LANES),
                  pl.BlockSpec((None, None, 2 * LANES, lw), lambda b, pt: (layer, b, 0, 0)),
                  per_seq(1, 2 * LANES), per_seq(1, LANES),
                  pl.BlockSpec(gexp.shape, lambda b, pt: (0, 0, 0))]
                 + [page_spec(p) for p in range(n_pages)],
        out_specs=[per_seq(1, MIXER_WIDTH), pl.BlockSpec((None, 2 * LANES, lw), lambda b, pt: (b, 0, 0))],
        scratch_shapes=[pltpu.VMEM((NB_PAD, 2 * LANES), F32)],
    )
    return pl.pallas_call(
        functools.partial(_nsa_sample_kernel, past=past),
        grid_spec=grid_spec,
        out_shape=[jax.ShapeDtypeStruct((s, 1, MIXER_WIDTH), F32),
                   jax.ShapeDtypeStruct((s, 2 * LANES, lw), F32)],
        compiler_params=_params("arbitrary"),
        name="nsa_sample",
    )(page_table.reshape(-1), q, cmp_pool, cs_new, win_t, win_new, gates, gexp, *([pool_t] * n_pages))


def _layer_norm_swish(y, g, b):
    mu = jnp.mean(y, axis=-1, keepdims=True)
    var = jnp.mean(jnp.square(y - mu), axis=-1, keepdims=True)
    z = (y - mu) * lax.rsqrt(var + EPS) * g + b
    return z * _sigmoid(z)


def _conv_prompt_kernel(a_ref, b_ref, ap_ref, bp_ref, w_ref, cb_ref, lg_ref, lb_ref, o_ref, st_ref, buf_ref):
    i = pl.program_id(1)
    glu = a_ref[...] * _sigmoid(b_ref[...])
    prev = ap_ref[...] * _sigmoid(bp_ref[...])
    buf_ref[0:CONV_HALO, :] = jnp.where(i > 0, prev, 0.0)
    buf_ref[CONV_HALO:, :] = glu
    tt = a_ref.shape[0]
    acc = jnp.zeros(glu.shape, F32)
    for k in range(CONV_WIDTH):
        acc = acc + w_ref[k:k + 1, :] * buf_ref[pl.ds(CONV_HALO - (CONV_WIDTH - 1) + k, tt), :]
    o_ref[...] = _layer_norm_swish(acc + cb_ref[...], lg_ref[...], lb_ref[...])

    @pl.when(i == pl.num_programs(1) - 1)
    def _():
        st_ref[0] = glu[tt - CONV_HALO:, :]


def conv_prompt(u, w, cb, lg, lb, n_batch):
    n = u.shape[0]
    t = n // n_batch
    tt = min(CONV_TILE, t)
    nt = t // tt
    ch = MIXER_WIDTH
    ratio = tt // CONV_HALO
    cur = lambda col: pl.BlockSpec((tt, ch), lambda b, i: (b * nt + i, col))
    prev = lambda col: pl.BlockSpec((CONV_HALO, ch), lambda b, i: (jnp.maximum((b * nt + i) * ratio - 1, 0), col))
    const = lambda *shape: pl.BlockSpec(shape, lambda b, i: (0,) * len(shape))
    return pl.pallas_call(
        _conv_prompt_kernel,
        grid=(n_batch, nt),
        in_specs=[cur(0), cur(1), prev(0), prev(1), const(CONV_HALO, ch), const(1, ch), const(1, ch), const(1, ch)],
        out_specs=[pl.BlockSpec((tt, ch), lambda b, i: (b * nt + i, 0)),
                   pl.BlockSpec((1, CONV_HALO, ch), lambda b, i: (b, 0, 0))],
        out_shape=[jax.ShapeDtypeStruct((n, ch), F32),
                   jax.ShapeDtypeStruct((n_batch, CONV_HALO, ch), F32)],
        scratch_shapes=[pltpu.VMEM((CONV_HALO + tt, ch), F32)],
        compiler_params=_params("parallel", "arbitrary"),
        name="conv_prompt",
    )(u, u, u, u, w, cb, lg, lb)


def _conv_sample_kernel(a_ref, b_ref, st_ref, w_ref, cb_ref, lg_ref, lb_ref, o_ref, glu_ref):
    glu = a_ref[...] * _sigmoid(b_ref[...])
    acc = w_ref[CONV_WIDTH - 1:CONV_WIDTH, :] * glu
    for k in range(CONV_WIDTH - 1):
        acc = acc + w_ref[k:k + 1, :] * st_ref[k]
    o_ref[...] = _layer_norm_swish(acc + cb_ref[...], lg_ref[...], lb_ref[...])
    glu_ref[...] = glu


def conv_sample(u, state_t, layer, w, cb, lg, lb):
    s = u.shape[0]
    ch = MIXER_WIDTH
    const = lambda *shape: pl.BlockSpec(shape, lambda i: (0,) * len(shape))
    return pl.pallas_call(
        _conv_sample_kernel,
        grid=(1,),
        in_specs=[pl.BlockSpec((s, ch), lambda i: (0, 0)), pl.BlockSpec((s, ch), lambda i: (0, 1)),
                  pl.BlockSpec((None, CONV_WIDTH - 1, s, ch), lambda i: (layer, 0, 0, 0)),
                  const(CONV_HALO, ch), const(1, ch), const(1, ch), const(1, ch)],
        out_specs=[const(s, ch), const(s, ch)],
        out_shape=[jax.ShapeDtypeStruct((s, ch), F32), jax.ShapeDtypeStruct((s, ch), F32)],
        compiler_params=_params("arbitrary"),
        name="conv_sample",
    )(u, u, state_t, w, cb, lg, lb)


SB_PAIRS = MIXER_WIDTH // LANES


def _stick_weights(z, valid, later, tri):
    sp = _softplus(z)
    log_stay = jnp.where(valid, -sp, 0.0)
    within = _dot_x2(log_stay, tri)
    a = jnp.where(valid, jnp.exp(z - sp + within + later), 0.0)
    return a, within[:, 0:1] + log_stay[:, 0:1]


def _sb_attn_kernel(q_ref, k_ref, v_ref, tri_ref, o_ref, acc_ref):
    qt = pl.program_id(1)
    t0 = qt * Q_TILE
    low = _iota((1, LANES), 1) < HEAD_DIM
    tpos = t0 + (_iota((2 * Q_TILE, 1), 0) & (Q_TILE - 1))
    tri = tri_ref[...]
    qz = []
    for pr in range(SB_PAIRS):
        q2 = q_ref[:, pr * LANES:(pr + 1) * LANES] * SCORE_SCALE
        qz.append(jnp.concatenate([jnp.where(low, q2, 0.0), jnp.where(low, 0.0, q2)], axis=0).astype(BF16))
    acc_ref[...] = jnp.zeros(acc_ref.shape, F32)

    def alive(carry):
        return (carry[0] <= qt) & (carry[1] > 0)

    def chunk(carry):
        i, _, laters = carry
        k0 = pl.multiple_of((qt - i) * SB_CHUNK, SB_CHUNK)
        valid = (k0 + _iota((1, SB_CHUNK), 1)) < tpos
        new = []
        for pr in range(SB_PAIRS):
            sl = slice(pr * LANES, (pr + 1) * LANES)
            z = lax.dot_general(qz[pr], k_ref[pl.ds(k0, SB_CHUNK), sl], NT_DIMS, preferred_element_type=F32)
            a, total = _stick_weights(z, valid, laters[pr], tri)
            acc_ref[pr] += jnp.dot(a.astype(BF16), v_ref[pl.ds(k0, SB_CHUNK), sl], preferred_element_type=F32)
            new.append(laters[pr] + total)
        live = jnp.max(functools.reduce(jnp.maximum, new)) > UNDERFLOW_LOG
        return i + 1, live.astype(jnp.int32), tuple(new)

    init = (jnp.int32(0), jnp.int32(1), tuple(jnp.zeros((2 * Q_TILE, 1), F32) for _ in range(SB_PAIRS)))
    lax.while_loop(alive, chunk, init)
    for pr in range(SB_PAIRS):
        o_ref[:, pr * LANES:(pr + 1) * LANES] = jnp.where(low, acc_ref[pr, :Q_TILE], acc_ref[pr, Q_TILE:])


def sb_attn_prompt(u, kv16, tri, n_batch):
    n = u.shape[0]
    t = n // n_batch
    nt = t // Q_TILE
    return pl.pallas_call(
        _sb_attn_kernel,
        grid=(n_batch, nt),
        in_specs=[pl.BlockSpec((Q_TILE, MIXER_WIDTH), lambda b, i: (b * nt + i, 0)),
                  pl.BlockSpec((t, MIXER_WIDTH), lambda b, i: (b, 0)),
                  pl.BlockSpec((t, MIXER_WIDTH), lambda b, i: (b, 1)),
                  pl.BlockSpec(tri.shape, lambda b, i: (0, 0))],
        out_specs=pl.BlockSpec((Q_TILE, MIXER_WIDTH), lambda b, i: (b * nt + i, 0)),
        out_shape=jax.ShapeDtypeStruct((n, MIXER_WIDTH), F32),
        scratch_shapes=[pltpu.VMEM((SB_PAIRS, 2 * Q_TILE, LANES), F32)],
        compiler_params=_params("parallel", "parallel"),
        name="sb_attn_prompt",
    )(u, kv16, kv16, tri)


def _sb_sample_kernel(pt_ref, q_ref, tri_ref, suffix_ref, *refs):
    page_refs, o_ref = refs[:-1], refs[-1]
    n_pages = len(page_refs)
    head = _iota((SAMPLE_ROWS, 1), 0)
    lane_head = _iota((1, MIXER_WIDTH), 1) >> 6
    qz = (jnp.where(lane_head == head, jnp.broadcast_to(q_ref[0], (SAMPLE_ROWS, MIXER_WIDTH)), 0.0)
          * SCORE_SCALE).astype(BF16)
    z = jnp.concatenate(
        [jnp.dot(qz, page_refs[p][0:MIXER_WIDTH, :].astype(BF16), preferred_element_type=F32) for p in range(n_pages)],
        axis=0)
    sp = _softplus(z)
    log_stay = -sp
    within = _dot_x3(log_stay, tri_ref[...])
    total = jnp.broadcast_to(within[:, 0:1] + log_stay[:, 0:1], z.shape)
    later = _dot_w3(suffix_ref[...], total)
    a = jnp.exp(z - sp + within + later)
    acc = sum(_dot_nt(a[p * SAMPLE_ROWS:(p + 1) * SAMPLE_ROWS], page_refs[p][MIXER_WIDTH:2 * MIXER_WIDTH, :])
              for p in range(n_pages))
    o_ref[0] = jnp.sum(jnp.where(lane_head == head, acc, 0.0), axis=0, keepdims=True)


def sb_attn_sample(page_table, layer, q, pool_t, tri, suffix):
    s, n_pages = page_table.shape
    page_spec = lambda p: pl.BlockSpec((None, None, 2 * MIXER_WIDTH, PAGE_SIZE),
                                       lambda b, pt: (layer, pt[b * n_pages + p], 0, 0))
    grid_spec = pltpu.PrefetchScalarGridSpec(
        num_scalar_prefetch=1,
        grid=(s,),
        in_specs=[pl.BlockSpec((1, 1, MIXER_WIDTH), lambda b, pt: (b, 0, 0)),
                  pl.BlockSpec(tri.shape, lambda b, pt: (0, 0)),
                  pl.BlockSpec(suffix.shape, lambda b, pt: (0, 0))]
                 + [page_spec(p) for p in range(n_pages)],
        out_specs=pl.BlockSpec((1, 1, MIXER_WIDTH), lambda b, pt: (b, 0, 0)),
    )
    return pl.pallas_call(
        _sb_sample_kernel,
        grid_spec=grid_spec,
        out_shape=jax.ShapeDtypeStruct((s, 1, MIXER_WIDTH), F32),
        compiler_params=_params("arbitrary"),
        name="sb_attn_sample",
    )(page_table.reshape(-1), q, tri, suffix, *([pool_t] * n_pages))


def _rope_tables(positions):
    half = ROT_DIM // 2
    inv = np.exp(-math.log(ROPE_THETA) * np.arange(0, ROT_DIM, 2, dtype=np.float64) / ROT_DIM)
    ang = np.asarray(positions, np.float64)[:, None] * inv[None, :]
    lane = np.arange(LANES) % HEAD_DIM
    cos = np.ones((len(positions), LANES))
    sa = np.zeros((len(positions), LANES))
    sb = np.zeros((len(positions), LANES))
    for l in range(LANES):
        r = lane[l]
        if r < half:
            cos[:, l] = np.cos(ang[:, r])
            sb[:, l] = -np.sin(ang[:, r])
        elif r < ROT_DIM:
            cos[:, l] = np.cos(ang[:, r - half])
            sa[:, l] = np.sin(ang[:, r - half])
    return tuple(jnp.asarray(a, F32) for a in (cos, sa, sb))


def _segment_ones():
    lane = np.arange(LANES)
    return jnp.asarray(lane[:, None] // HEAD_DIM == lane[None, :] // HEAD_DIM, BF16)


def _gate_expand():
    e = np.zeros((3, LANES, MIXER_WIDTH), np.float32)
    for j in range(3):
        for h in range(NSA_HEADS):
            e[j, 3 * h + j, h * HEAD_DIM:(h + 1) * HEAD_DIM] = 1.0
    return jnp.asarray(e, BF16)


def _later_ones(n):
    idx = np.arange(n)
    return jnp.asarray(idx[:, None] > idx[None, :], BF16)


def _later_pages(n_pages):
    p = np.repeat(np.arange(n_pages), SAMPLE_ROWS)
    r = np.tile(np.arange(SAMPLE_ROWS), n_pages)
    return jnp.asarray((r[:, None] == r[None, :]) & (p[None, :] > p[:, None]), BF16)


def _compress_weights(w_cmp):
    w = w_cmp.reshape(2, NSA_BLOCK, HEAD_DIM, HEAD_DIM)
    out = jnp.zeros((2, NSA_BLOCK, NSA_GROUPS, HEAD_DIM, NSA_GROUPS, HEAD_DIM), F32)
    for g in range(NSA_GROUPS):
        out = out.at[:, :, g, :, g, :].set(w)
    return out.reshape(2, NSA_BLOCK, LANES, LANES).astype(BF16)


def _compress_weights_t(w_cmp):
    w = w_cmp.reshape(2, NSA_BLOCK, HEAD_DIM, HEAD_DIM).transpose(0, 2, 1, 3)
    halves = PAGE_SIZE // NSA_BLOCK
    out = jnp.zeros((2, HEAD_DIM, halves, NSA_BLOCK, halves, HEAD_DIM), F32)
    for h in range(halves):
        out = out.at[:, :, h, :, h, :].set(w)
    return out.reshape(2, HEAD_DIM, PAGE_SIZE, halves * HEAD_DIM).astype(BF16)


def _rows_last(x):
    a, b, rows = x.shape[:3]
    perm = (0, 1) + tuple(range(3, x.ndim)) + (2,)
    return x.transpose(perm).reshape(a, b, -1, rows)


def kernel(x_prompt, x_sample, cache_nsa_kv, state_nsa_win, state_conv, cache_sb_kv, cache_mem_kv, page_table,
           mem_prompt, norm_mix, norm_ffn, norm_mem, w_in_nsa, w_in_conv, w_in_sb, w_out, w_mem_kv, qk_norm_nsa,
           qk_norm_mem, w_nsa_cmp, conv_w, conv_b, conv_ln_g, conv_ln_b, w_ffn_in, w_ffn_out):
    n_batch, seq, d_model = x_prompt.shape
    n_dec = x_sample.shape[0]
    depth = w_out.shape[0]
    n_pages = page_table.shape[1]
    past = n_pages * PAGE_SIZE
    n_mem = mem_prompt.shape[1]
    ffn_hidden = w_ffn_out.shape[1]
    assert x_sample.shape[1] == 1 and seq % CONV_TILE == 0 and seq >= NSA_WINDOW + Q_TILE
    assert past % NSA_BLOCK == 0 and ffn_hidden % LANES == 0 and n_dec % MEM_BATCH == 0
    assert PAGE_SIZE == SB_CHUNK == LANES and state_nsa_win.shape[2] % LANES == 0

    xp = x_prompt.reshape(n_batch * seq, d_model)
    xs = x_sample.reshape(n_dec, d_model)

    seg = _segment_ones()
    gexp = _gate_expand()
    tri = _later_ones(SB_CHUNK)
    suffix = _later_pages(n_pages)
    rope_p = _rope_tables(np.arange(seq))
    rope_s = _rope_tables(np.full((n_dec,), past))

    nsa_pool_t = _rows_last(cache_nsa_kv)
    nsa_win_t = _rows_last(state_nsa_win)
    sb_pool_t = _rows_last(cache_sb_kv)
    mem_cache_t = _rows_last(cache_mem_kv)
    conv_state_t = state_conv.transpose(0, 2, 1, 3)

    nq, nkv = MIXER_WIDTH, 6 * NSA_GROUPS * HEAD_DIM
    n_gate = 3 * NSA_HEADS
    w_nsa = jnp.concatenate(
        [w_in_nsa[:, :, :nq + nkv], w_in_nsa[:, :, nq + nkv + n_gate:], w_in_nsa[:, :, nq + nkv:nq + nkv + n_gate],
         jnp.zeros(w_in_nsa.shape[:2] + (LANES - n_gate,), F32)], axis=-1).astype(BF16)
    w_conv = w_in_conv.astype(BF16)
    w_sb = w_in_sb.astype(BF16)
    w_o1 = w_out[:, :MIXER_WIDTH].astype(BF16)
    w_o2 = w_out[:, MIXER_WIDTH:].astype(BF16)
    w_gate_up = w_ffn_in.astype(BF16)
    w_down = w_ffn_out.astype(BF16)

    gq_mem = jnp.tile(qk_norm_mem[:, 0], (1, LANES // HEAD_DIM))
    gk_mem = jnp.tile(qk_norm_mem[:, 1], (1, LANES // HEAD_DIM))
    mem_kv = mem_kv_all(mem_prompt.reshape(n_batch * n_mem, d_model), norm_mem, w_mem_kv.astype(BF16), gk_mem, seg)

    nsa_p, nsa_s, win_p, win_s, conv_p, conv_s, sb_p, sb_s = [], [], [], [], [], [], [], []
    for layer in range(depth):
        kind, j = layer % N_MIXERS, layer // N_MIXERS
        if kind == 0:
            up = proj_in(xp, norm_mix[layer], w_nsa[j])
            us = proj_in(xs, norm_mix[layer], w_nsa[j])
            mem_col = 2 * MIXER_WIDTH // MEM_WIDTH
            gq = jnp.tile(qk_norm_nsa[j, 0], LANES // HEAD_DIM).reshape(1, LANES)
            gkv = jnp.tile(qk_norm_nsa[j, 1:4], (1, LANES // HEAD_DIM)).reshape(3, 1, LANES)

            q_p, cs_p, wn_p, gt_p = nsa_prep(up, rope_p, gq, gkv, seg, seq // 256, 256)
            cmp_p = compress(cs_p, _compress_weights(w_nsa_cmp[j]))
            o_p = nsa_attn_prompt(q_p, gt_p, cs_p, wn_p, cmp_p, gexp, n_batch)

            q_s, cs_s, wn_s, gt_s = nsa_prep(us, rope_s, gq, gkv, seg, 1, n_dec)
            cmp_pool = compress_pool(nsa_pool_t, j, _compress_weights_t(w_nsa_cmp[j]))
            o_s, win_new_t = nsa_sample(page_table, j, q_s.reshape(n_dec, 1, MIXER_WIDTH), cmp_pool,
                                        cs_s.reshape(n_dec, 1, 4 * LANES), nsa_win_t,
                                        wn_s.reshape(n_dec, 1, 2 * LANES), gt_s.reshape(n_dec, 1, LANES), gexp,
                                        nsa_pool_t)
            o_s = o_s.reshape(n_dec, MIXER_WIDTH)

            lw = nsa_win_t.shape[-1]
            nsa_p.append(cs_p.reshape(n_batch, seq, 4, NSA_GROUPS, HEAD_DIM))
            nsa_s.append(cs_s.reshape(n_dec, 1, 4, NSA_GROUPS, HEAD_DIM))
            win_p.append(wn_p.reshape(n_batch, seq, 2, NSA_GROUPS, HEAD_DIM)[:, seq - min(NSA_WINDOW, seq):])
            win_s.append(win_new_t.reshape(n_dec, 2, NSA_GROUPS, HEAD_DIM, lw).transpose(0, 4, 1, 2, 3))
        elif kind == 1:
            up = proj_in(xp, norm_mix[layer], w_conv[j])
            us = proj_in(xs, norm_mix[layer], w_conv[j])
            mem_col = 2 * MIXER_WIDTH // MEM_WIDTH
            cw = jnp.concatenate([conv_w[j], jnp.zeros((CONV_HALO - CONV_WIDTH, MIXER_WIDTH), F32)], axis=0)
            vecs = [v[j].reshape(1, MIXER_WIDTH) for v in (conv_b, conv_ln_g, conv_ln_b)]
            o_p, tail = conv_prompt(up, cw, *vecs, n_batch)
            o_s, glu_s = conv_sample(us, conv_state_t, j, cw, *vecs)
            conv_p.append(tail[:, CONV_HALO - (CONV_WIDTH - 1):])
            conv_s.append(jnp.concatenate([conv_state_t[j, 1:], glu_s[None]], axis=0).transpose(1, 0, 2))
        else:
            up, kv16 = proj_in(xp, norm_mix[layer], w_sb[j], narrow_cols=(MIXER_WIDTH, 3 * MIXER_WIDTH))
            us = proj_in(xs, norm_mix[layer], w_sb[j])
            mem_col = 3 * MIXER_WIDTH // MEM_WIDTH
            o_p = sb_attn_prompt(up, kv16, tri, n_batch)
            o_s = sb_attn_sample(page_table, j, us[:, :MIXER_WIDTH].reshape(n_dec, 1, MIXER_WIDTH), sb_pool_t, tri,
                                 suffix).reshape(n_dec, MIXER_WIDTH)
            sb_p.append(up[:, MIXER_WIDTH:3 * MIXER_WIDTH].reshape(n_batch, seq, 2, NSA_HEADS, HEAD_DIM))
            sb_s.append(us[:, MIXER_WIDTH:3 * MIXER_WIDTH].reshape(n_dec, 1, 2, NSA_HEADS, HEAD_DIM))

        gq_l = gq_mem[layer].reshape(1, LANES)
        om_p = mem_attn_prompt(up, mem_col, mem_kv[layer], gq_l, seg, n_batch)
        om_s = mem_attn_sample(us[:, mem_col * MEM_WIDTH:(mem_col + 1) * MEM_WIDTH], mem_cache_t, layer, gq_l, seg)

        ffn_w = (w_o1[layer], w_o2[layer], norm_ffn[layer], w_gate_up[layer], w_down[layer])
        xp = out_ffn(o_p, om_p, xp, *ffn_w)
        xs = out_ffn(o_s, om_s, xs, *ffn_w)

    return (xp.reshape(x_prompt.shape), xs.reshape(x_sample.shape),
            jnp.stack(nsa_p), jnp.stack(nsa_s), jnp.stack(win_p), jnp.stack(win_s),
            jnp.stack(conv_p), jnp.stack(conv_s), jnp.stack(sb_p), jnp.stack(sb_s),
            mem_kv.reshape(depth, n_batch, n_mem, 2, MEM_HEADS, HEAD_DIM))
```

```python
import functools
import math

import numpy as np
import jax
import jax.numpy as jnp
from jax import lax
from jax.experimental import pallas as pl
from jax.experimental.pallas import tpu as pltpu

F32 = jnp.float32
BF16 = jnp.bfloat16

HEAD_DIM = 64
MEM_HEADS = 4
MEM_WIDTH = MEM_HEADS * HEAD_DIM
N_MIXERS = 3
NSA_HEADS = 12
NSA_GROUPS = 2
NSA_HPG = NSA_HEADS // NSA_GROUPS
NSA_BLOCK = 64
NSA_TOPK = 16
NSA_WINDOW = 512
MIXER_WIDTH = NSA_HEADS * HEAD_DIM
CONV_WIDTH = 31
ROPE_THETA = 500000.0
ROT_DIM = HEAD_DIM // 4
EPS = 1e-6
PAGE_SIZE = 128

LANES = 128
SUBLANES = 8
VMEM_LIMIT = 56 * 1024 * 1024

ROW_TILE = 512
FFN_TILE = 256
Q_TILE = 128
SEL_CHUNK = 512
SB_CHUNK = 128
CONV_TILE = 512
CMP_BLOCKS = 256
CMP_PAGES = 128
MEM_BATCH = 8
CONV_HALO = 32

SCORE_SCALE = HEAD_DIM ** -0.5
BIG = 1e30
UNDERFLOW_LOG = -104.0
NT_DIMS = (((1,), (1,)), ((), ()))


def _params(*sem):
    return pltpu.CompilerParams(dimension_semantics=sem, vmem_limit_bytes=VMEM_LIMIT)


def _pick_tile(n, cap):
    return max(t for t in range(SUBLANES, cap + 1, SUBLANES) if n % t == 0)


def _iota(shape, dim):
    return lax.broadcasted_iota(jnp.int32, shape, dim)


def _dot(a, b):
    return jnp.dot(a.astype(BF16), b.astype(BF16), preferred_element_type=F32)


def _dot_nt(a, b):
    return lax.dot_general(a.astype(BF16), b.astype(BF16), NT_DIMS, preferred_element_type=F32)


def _split2(x):
    hi = x.astype(BF16)
    lo = (x - hi.astype(F32)).astype(BF16)
    return hi, lo


def _split3(x):
    hi = x.astype(BF16)
    r = x - hi.astype(F32)
    mid = r.astype(BF16)
    lo = (r - mid.astype(F32)).astype(BF16)
    return hi, mid, lo


def _dot_x2(x, w_bf16):
    return sum(jnp.dot(part, w_bf16, preferred_element_type=F32) for part in _split2(x))


def _dot_x3(x, w_bf16):
    return sum(jnp.dot(part, w_bf16, preferred_element_type=F32) for part in _split3(x))


def _dot_w3(w_bf16, x):
    return sum(jnp.dot(w_bf16, part, preferred_element_type=F32) for part in _split3(x))


def _dot_nt_precise(a, b):
    ah, al = _split2(a)
    bh, bl = _split2(b)
    return (lax.dot_general(ah, bh, NT_DIMS, preferred_element_type=F32)
            + lax.dot_general(ah, bl, NT_DIMS, preferred_element_type=F32)
            + lax.dot_general(al, bh, NT_DIMS, preferred_element_type=F32))


def _bf16_round(x):
    return x.astype(BF16).astype(F32)


def _rms_rows(x, g):
    ms = jnp.mean(x * x, axis=-1, keepdims=True)
    return x * lax.rsqrt(ms + EPS) * g


def _head_rms(x, seg_ones, g):
    ms = _dot_x2(x * x, seg_ones) * (1.0 / HEAD_DIM)
    return x * lax.rsqrt(ms + EPS) * g


def _sigmoid(x):
    return 1.0 / (1.0 + jnp.exp(-x))


def _softplus(z):
    return jnp.maximum(z, 0.0) + jnp.log1p(jnp.exp(-jnp.abs(z)))


def _proj_in_kernel(x_ref, g_ref, w_ref, o_ref, *extra_refs, narrow_cols):
    h = _rms_rows(x_ref[...], g_ref[...]).astype(BF16)
    u = jnp.dot(h, w_ref[...], preferred_element_type=F32)
    o_ref[...] = u
    if narrow_cols is not None:
        narrow_ref, rows_last_ref = extra_refs
        narrow_ref[...] = u[:, narrow_cols[0]:narrow_cols[1]].astype(BF16)
        for c in range((narrow_cols[1] - narrow_cols[0]) // LANES):
            lo = narrow_cols[0] + c * LANES
            rows_last_ref[c * LANES:(c + 1) * LANES, :] = u[:, lo:lo + LANES].T


def proj_in(x, g, w, narrow_cols=None, n_batch=None):
    n, d = x.shape
    m = w.shape[1]
    tm = min(ROW_TILE, n)
    out_specs = [pl.BlockSpec((tm, m), lambda i: (i, 0))]
    out_shape = [jax.ShapeDtypeStruct((n, m), F32)]
    if narrow_cols is not None:
        width = narrow_cols[1] - narrow_cols[0]
        seq = n // n_batch
        nt = seq // tm
        out_specs.append(pl.BlockSpec((tm, width), lambda i: (i, 0)))
        out_shape.append(jax.ShapeDtypeStruct((n, width), BF16))
        out_specs.append(pl.BlockSpec((None, width, tm), lambda i: (i // nt, 0, i % nt)))
        out_shape.append(jax.ShapeDtypeStruct((n_batch, width, seq), F32))
    res = pl.pallas_call(
        functools.partial(_proj_in_kernel, narrow_cols=narrow_cols),
        grid=(n // tm,),
        in_specs=[pl.BlockSpec((tm, d), lambda i: (i, 0)),
                  pl.BlockSpec((1, d), lambda i: (0, 0)),
                  pl.BlockSpec((d, m), lambda i: (0, 0))],
        out_specs=out_specs,
        out_shape=out_shape,
        compiler_params=_params("parallel"),
        name="proj_in",
    )(x, g.reshape(1, d), w)
    return res if narrow_cols is not None else res[0]


def _out_ffn_kernel(o_ref, om_ref, x_ref, w1_ref, w2_ref, gf_ref, wgu_ref, wo_ref, y_ref):
    x = x_ref[...] + _dot(o_ref[...], w1_ref[...]) + _dot(om_ref[...], w2_ref[...])
    h = _rms_rows(x, gf_ref[...]).astype(BF16)
    hidden = wo_ref.shape[0]
    gate_up = jnp.dot(h, wgu_ref[...], preferred_element_type=F32)
    gate, up = gate_up[:, :hidden], gate_up[:, hidden:]
    act = (gate * _sigmoid(gate) * up).astype(BF16)
    y_ref[...] = x + jnp.dot(act, wo_ref[...], preferred_element_type=F32)


def out_ffn(o, om, x, w1, w2, gf, wgu, wo):
    n, d = x.shape
    tm = min(FFN_TILE, n)
    const = lambda *shape: pl.BlockSpec(shape, lambda i: (0,) * len(shape), pipeline_mode=pl.Buffered(1))
    return pl.pallas_call(
        _out_ffn_kernel,
        grid=(n // tm,),
        in_specs=[pl.BlockSpec((tm, o.shape[1]), lambda i: (i, 0)),
                  pl.BlockSpec((tm, om.shape[1]), lambda i: (i, 0)),
                  pl.BlockSpec((tm, d), lambda i: (i, 0)),
                  const(*w1.shape), const(*w2.shape), const(1, d), const(*wgu.shape), const(*wo.shape)],
        out_specs=pl.BlockSpec((tm, d), lambda i: (i, 0)),
        out_shape=jax.ShapeDtypeStruct((n, d), F32),
        compiler_params=_params("parallel"),
        name="out_ffn",
    )(o, om, x, w1, w2, gf.reshape(1, d), wgu, wo)


def _mem_kv_kernel(mem_ref, gn_ref, w_ref, gk_ref, seg_ref, o_ref):
    h = _rms_rows(mem_ref[...], gn_ref[0]).astype(BF16)
    kv = jnp.dot(h, w_ref[0], preferred_element_type=F32)
    seg = seg_ref[...]
    for c in range(MEM_WIDTH // LANES):
        sl = slice(c * LANES, (c + 1) * LANES)
        o_ref[0, :, sl] = _head_rms(kv[:, sl], seg, gk_ref[0])
    o_ref[0, :, MEM_WIDTH:] = kv[:, MEM_WIDTH:]


def mem_kv_all(mem, g_norm, w_kv, g_k, seg):
    n, d = mem.shape
    nl = w_kv.shape[0]
    return pl.pallas_call(
        _mem_kv_kernel,
        grid=(nl,),
        in_specs=[pl.BlockSpec((n, d), lambda l: (0, 0)),
                  pl.BlockSpec((1, 1, d), lambda l: (l, 0, 0)),
                  pl.BlockSpec((1, d, 2 * MEM_WIDTH), lambda l: (l, 0, 0)),
                  pl.BlockSpec((1, 1, LANES), lambda l: (l, 0, 0)),
                  pl.BlockSpec((LANES, LANES), lambda l: (0, 0))],
        out_specs=pl.BlockSpec((1, n, 2 * MEM_WIDTH), lambda l: (l, 0, 0)),
        out_shape=jax.ShapeDtypeStruct((nl, n, 2 * MEM_WIDTH), F32),
        compiler_params=_params("arbitrary"),
        name="mem_kv",
    )(mem, g_norm.reshape(nl, 1, d), w_kv, g_k.reshape(nl, 1, LANES), seg)


def _mem_query(qm, seg, gq):
    parts = [_head_rms(qm[:, c * LANES:(c + 1) * LANES], seg, gq) for c in range(MEM_WIDTH // LANES)]
    return jnp.concatenate(parts, axis=1) * SCORE_SCALE


def _softmax_rows(s):
    p = jnp.exp(s - jnp.max(s, axis=-1, keepdims=True))
    return p / jnp.sum(p, axis=-1, keepdims=True)


def _mem_attn_kernel(qm_ref, kv_ref, gq_ref, seg_ref, o_ref):
    qn = _mem_query(qm_ref[...], seg_ref[...], gq_ref[...])
    r = qn.shape[0]
    head_of_lane = _iota((1, MEM_WIDTH), 1) >> 6
    qz = jnp.concatenate([jnp.where(head_of_lane == h, qn, 0.0) for h in range(MEM_HEADS)], axis=0)
    p = _softmax_rows(_dot_nt(qz, kv_ref[:, :MEM_WIDTH]))
    res = _dot(p, kv_ref[:, MEM_WIDTH:])
    out = jnp.zeros((r, MEM_WIDTH), F32)
    for h in range(MEM_HEADS):
        out = out + jnp.where(head_of_lane == h, res[h * r:(h + 1) * r], 0.0)
    o_ref[...] = out


def mem_attn_prompt(u, mem_col, kv, gq, seg, n_batch):
    n = u.shape[0]
    t = n // n_batch
    tm = 256
    n_mem = kv.shape[0] // n_batch
    nt = t // tm
    return pl.pallas_call(
        _mem_attn_kernel,
        grid=(n_batch, nt),
        in_specs=[pl.BlockSpec((tm, MEM_WIDTH), lambda b, i: (b * nt + i, mem_col)),
                  pl.BlockSpec((n_mem, 2 * MEM_WIDTH), lambda b, i: (b, 0)),
                  pl.BlockSpec((1, LANES), lambda b, i: (0, 0)),
                  pl.BlockSpec((LANES, LANES), lambda b, i: (0, 0))],
        out_specs=pl.BlockSpec((tm, MEM_WIDTH), lambda b, i: (b * nt + i, 0)),
        out_shape=jax.ShapeDtypeStruct((n, MEM_WIDTH), F32),
        compiler_params=_params("parallel", "parallel"),
        name="mem_attn_prompt",
    )(u, kv, gq, seg)


def _mem_attn_sample_kernel(qm_ref, kv_ref, gq_ref, seg_ref, o_ref):
    qn = _mem_query(qm_ref[0], seg_ref[...], gq_ref[...])
    row = _iota((SUBLANES, 1), 0)
    head_of_lane = _iota((1, MEM_WIDTH), 1) >> 6
    out = jnp.zeros((MEM_BATCH, MEM_WIDTH), F32)
    for i in range(MEM_BATCH):
        qz = jnp.where(head_of_lane == (row & (MEM_HEADS - 1)), jnp.broadcast_to(qn[i:i + 1], (SUBLANES, MEM_WIDTH)), 0.0)
        p = _softmax_rows(_dot(qz, kv_ref[i, 0:MEM_WIDTH, :]))
        res = _dot_nt(p, kv_ref[i, MEM_WIDTH:2 * MEM_WIDTH, :])
        flat = jnp.sum(jnp.where(head_of_lane == row, res, 0.0), axis=0, keepdims=True)
        out = out + jnp.where(_iota((MEM_BATCH, 1), 0) == i, flat, 0.0)
    o_ref[0] = out


def mem_attn_sample(qm, kv_t, layer, gq, seg):
    _, s, _, n_mem = kv_t.shape
    nb = s // MEM_BATCH
    return pl.pallas_call(
        _mem_attn_sample_kernel,
        grid=(nb,),
        in_specs=[pl.BlockSpec((1, MEM_BATCH, MEM_WIDTH), lambda b: (b, 0, 0)),
                  pl.BlockSpec((None, MEM_BATCH, 2 * MEM_WIDTH, n_mem), lambda b: (layer, b, 0, 0)),
                  pl.BlockSpec((1, LANES), lambda b: (0, 0)),
                  pl.BlockSpec((LANES, LANES), lambda b: (0, 0))],
        out_specs=pl.BlockSpec((1, MEM_BATCH, MEM_WIDTH), lambda b: (b, 0, 0)),
        out_shape=jax.ShapeDtypeStruct((nb, MEM_BATCH, MEM_WIDTH), F32),
        compiler_params=_params("parallel"),
        name="mem_attn_sample",
    )(qm.reshape(nb, MEM_BATCH, MEM_WIDTH), kv_t, gq, seg).reshape(s, MEM_WIDTH)


def _nsa_prep_kernel(q_ref, kv_ref, gt_ref, cos_ref, sa_ref, sb_ref, gq_ref, gkv_ref, seg_ref,
                     qo_ref, cso_ref, wino_ref, go_ref, *rows_last_refs):
    cos, sa, sb = cos_ref[...], sa_ref[...], sb_ref[...]
    seg = seg_ref[...]
    half = ROT_DIM // 2

    def norm_rope(x, g):
        y = _head_rms(x, seg, g)
        return y * cos + pltpu.roll(y, half, 1) * sa + pltpu.roll(y, LANES - half, 1) * sb

    for c in range(MIXER_WIDTH // LANES):
        sl = slice(c * LANES, (c + 1) * LANES)
        qo_ref[:, sl] = norm_rope(q_ref[:, sl], gq_ref[...])
    for c in range(6):
        x = kv_ref[:, c * LANES:(c + 1) * LANES]
        if c % 2 == 0:
            x = norm_rope(x, gkv_ref[c // 2])
        dst, cc = (cso_ref, c) if c < 4 else (wino_ref, c - 4)
        dst[:, cc * LANES:(cc + 1) * LANES] = x
        if rows_last_refs:
            dst_t = rows_last_refs[0] if c < 4 else rows_last_refs[1]
            dst_t[cc * LANES:(cc + 1) * LANES, :] = x.T
    go_ref[...] = _sigmoid(gt_ref[...])


def nsa_prep(u, rope_tabs, gq, gkv, seg, n_time_tiles, tm, rows_last=False):
    n = u.shape[0]
    cos, sa, sb = rope_tabs
    tab = pl.BlockSpec((tm, LANES), lambda i: (i % n_time_tiles, 0))
    const = lambda *shape: pl.BlockSpec(shape, lambda i: (0,) * len(shape))
    out_specs = [pl.BlockSpec((tm, MIXER_WIDTH), lambda i: (i, 0)),
                 pl.BlockSpec((tm, 4 * LANES), lambda i: (i, 0)),
                 pl.BlockSpec((tm, 2 * LANES), lambda i: (i, 0)),
                 pl.BlockSpec((tm, LANES), lambda i: (i, 0))]
    out_shape = [jax.ShapeDtypeStruct((n, MIXER_WIDTH), F32),
                 jax.ShapeDtypeStruct((n, 4 * LANES), F32),
                 jax.ShapeDtypeStruct((n, 2 * LANES), F32),
                 jax.ShapeDtypeStruct((n, LANES), F32)]
    if rows_last:
        n_batch, seq = n // (n_time_tiles * tm), n_time_tiles * tm
        for width in (4 * LANES, 2 * LANES):
            out_specs.append(pl.BlockSpec((None, width, tm), lambda i: (i // n_time_tiles, 0, i % n_time_tiles)))
            out_shape.append(jax.ShapeDtypeStruct((n_batch, width, seq), F32))
    return pl.pallas_call(
        _nsa_prep_kernel,
        grid=(n // tm,),
        in_specs=[pl.BlockSpec((tm, MIXER_WIDTH), lambda i: (i, 0)),
                  pl.BlockSpec((tm, MIXER_WIDTH), lambda i: (i, 1)),
                  pl.BlockSpec((tm, LANES), lambda i: (i, (2 * MIXER_WIDTH + MEM_WIDTH) // LANES)),
                  tab, tab, tab, const(1, LANES), const(3, 1, LANES), const(LANES, LANES)],
        out_specs=out_specs,
        out_shape=out_shape,
        compiler_params=_params("parallel"),
        name="nsa_prep",
    )(u, u, u, cos, sa, sb, gq, gkv, seg)


def _compress_kernel(xk_ref, xv_ref, w_ref, o_ref):
    mb = o_ref.shape[0]
    for c, x_ref in enumerate((xk_ref, xv_ref)):
        acc = jnp.zeros((mb, LANES), F32)
        for r in range(NSA_BLOCK):
            xr = x_ref[pl.ds(r, mb, stride=NSA_BLOCK), :]
            acc = acc + jnp.dot(xr.astype(BF16), w_ref[c, r], preferred_element_type=F32)
        o_ref[:, c * LANES:(c + 1) * LANES] = acc


def compress(cs, wbd):
    nblk = cs.shape[0] // NSA_BLOCK
    mb = _pick_tile(nblk, CMP_BLOCKS)
    return pl.pallas_call(
        _compress_kernel,
        grid=(nblk // mb,),
        in_specs=[pl.BlockSpec((mb * NSA_BLOCK, LANES), lambda i: (i, 0)),
                  pl.BlockSpec((mb * NSA_BLOCK, LANES), lambda i: (i, 1)),
                  pl.BlockSpec(wbd.shape, lambda i: (0, 0, 0, 0))],
        out_specs=pl.BlockSpec((mb, 2 * LANES), lambda i: (i, 0)),
        out_shape=jax.ShapeDtypeStruct((nblk, 2 * LANES), F32),
        compiler_params=_params("parallel"),
        name="nsa_compress",
    )(cs, cs, wbd)


def _compress_pool_kernel(x_ref, w_ref, o_ref):
    mbp = o_ref.shape[0]
    low = _iota((1, LANES), 1) < HEAD_DIM
    for c in range(2):
        slabs = []
        for g in range(NSA_GROUPS):
            acc = jnp.zeros((mbp, LANES), F32)
            for d in range(HEAD_DIM):
                xr = x_ref[:, c * LANES + g * HEAD_DIM + d, :]
                acc = acc + jnp.dot(xr.astype(BF16), w_ref[c, d], preferred_element_type=F32)
            slabs.append(acc)
        first = jnp.where(low, slabs[0], pltpu.roll(slabs[1], HEAD_DIM, 1))
        second = jnp.where(low, pltpu.roll(slabs[0], HEAD_DIM, 1), slabs[1])
        o_ref[:, c * LANES:(c + 1) * LANES] = first
        o_ref[:, (2 + c) * LANES:(3 + c) * LANES] = second


def compress_pool(pool_t, layer, wt):
    n_pool = pool_t.shape[1]
    mbp = _pick_tile(n_pool, CMP_PAGES)
    return pl.pallas_call(
        _compress_pool_kernel,
        grid=(n_pool // mbp,),
        in_specs=[pl.BlockSpec((None, mbp, 2 * LANES, PAGE_SIZE), lambda i: (layer, i, 0, 0)),
                  pl.BlockSpec(wt.shape, lambda i: (0, 0, 0, 0))],
        out_specs=pl.BlockSpec((mbp, 4 * LANES), lambda i: (i, 0)),
        out_shape=jax.ShapeDtypeStruct((n_pool, 4 * LANES), F32),
        compiler_params=_params("parallel"),
        name="nsa_compress_pool",
    )(pool_t, wt)


def _group_queries(q_chunks, g):
    parts = []
    for h in range(NSA_HPG):
        hh = g * NSA_HPG + h
        chunk = q_chunks(hh // 2)
        if hh % 2 != g:
            chunk = pltpu.roll(chunk, HEAD_DIM, 1)
        parts.append(chunk)
    qz = jnp.concatenate(parts, axis=0) * SCORE_SCALE
    return jnp.where((_iota((1, LANES), 1) >> 6) == g, qz, 0.0)


def _select_blocks(imp, cur):
    r, nb = imp.shape
    nidx = _iota((1, nb), 1)
    score = jnp.where(nidx == cur, BIG, jnp.where(nidx < cur, imp, -1.0))
    rank = jnp.zeros((r, nb), F32)
    for m in range(nb):
        col = score[:, m:m + 1]
        rank = rank + jnp.where(nidx > m, jnp.where(col >= score, 1.0, 0.0), jnp.where(col > score, 1.0, 0.0))
    return (rank < float(NSA_TOPK)) & (nidx <= cur)


def _select_blocks_t(imp, cur, live_blocks, score_ref):
    nb, r = imp.shape
    nidx = _iota((nb, 1), 0)
    score = jnp.where(nidx == cur, BIG, jnp.where(nidx < cur, imp, -1.0))
    score_ref[...] = score

    def count_ahead(m, rank):
        row = score_ref[pl.ds(m, 1), :]
        return rank + jnp.where(nidx > m, jnp.where(row >= score, 1.0, 0.0), jnp.where(row > score, 1.0, 0.0))

    rank = lax.fori_loop(0, live_blocks, count_ahead, jnp.zeros((nb, r), F32))
    return jnp.where((rank < float(NSA_TOPK)) & (nidx <= cur), 1.0, 0.0)


def _masked_softmax_parts(s, ok):
    m = jnp.max(jnp.where(ok, s, -BIG), axis=-1, keepdims=True)
    p = jnp.where(ok, jnp.exp(s - m), 0.0)
    return p, jnp.maximum(jnp.sum(p, axis=-1, keepdims=True), 1e-30)


def _place_heads(res, g, rows):
    low = _iota((1, LANES), 1) < HEAD_DIM
    out = []
    for c in range(NSA_HPG // 2):
        a = res[(2 * c) * rows:(2 * c + 1) * rows]
        b = res[(2 * c + 1) * rows:(2 * c + 2) * rows]
        if g == 1:
            a = pltpu.roll(a, HEAD_DIM, 1)
        else:
            b = pltpu.roll(b, HEAD_DIM, 1)
        out.append(jnp.where(low, a, b))
    return out


def _nsa_attn_kernel(q_ref, g_ref, cs_ref, win_ref, cmp_ref, cmpt_ref, gexp_ref, o_ref, score_ref):
    qt = pl.program_id(1)
    t0 = qt * Q_TILE
    seq = cs_ref.shape[0]
    nb = cmp_ref.shape[0]
    tpos = t0 + _iota((Q_TILE, 1), 0)
    rows = NSA_HPG * Q_TILE

    gates = g_ref[...]
    gexp = [_dot_x2(gates, gexp_ref[j]) for j in range(3)]

    win_span = min(NSA_WINDOW + Q_TILE, seq)
    w0 = pl.multiple_of(jnp.maximum(t0 + Q_TILE - win_span, 0), Q_TILE)
    n_chunks = (t0 + Q_TILE + SEL_CHUNK - 1) // SEL_CHUNK

    t_lane = t0 + (_iota((1, rows), 1) & (Q_TILE - 1))
    block_row = _iota((nb, 1), 0)
    complete = (block_row + 1) * NSA_BLOCK <= t_lane + 1
    cur_lane = (t0 + _iota((1, Q_TILE), 1)) >> 6
    live_blocks = jnp.minimum((t0 + Q_TILE - 1) // NSA_BLOCK + 1, nb)

    branch_out = []
    for g in range(NSA_GROUPS):
        qz = _group_queries(lambda c: q_ref[:, c * LANES:(c + 1) * LANES], g)
        qzb = qz.astype(BF16)

        s = _dot_nt_precise(cmp_ref[:, 0:LANES], qz)
        m_cmp = jnp.max(jnp.where(complete, s, -BIG), axis=0, keepdims=True)
        p = jnp.where(complete, jnp.exp(s - m_cmp), 0.0)
        p_cmp = p / jnp.maximum(jnp.sum(p, axis=0, keepdims=True), 1e-30)
        o_cmp_t = _dot(cmpt_ref[LANES:2 * LANES, :], p_cmp)
        importance = sum(p_cmp[:, h * Q_TILE:(h + 1) * Q_TILE] for h in range(NSA_HPG))
        sel_t = _select_blocks_t(importance, cur_lane, live_blocks, score_ref)
        sel = jnp.concatenate([sel_t, jnp.zeros((LANES - nb, Q_TILE), F32)], axis=0).T
        sel_bf = sel.astype(BF16)

        def sel_chunk(kc, carry):
            m, l, acc = carry
            k0 = pl.multiple_of(kc * SEL_CHUNK, SEL_CHUNK)
            kch = cs_ref[pl.ds(k0, SEL_CHUNK), 2 * LANES:3 * LANES]
            vch = cs_ref[pl.ds(k0, SEL_CHUNK), 3 * LANES:4 * LANES]
            kpos = k0 + _iota((1, SEL_CHUNK), 1)
            expand = jnp.where((kpos >> 6) == _iota((LANES, 1), 0), 1.0, 0.0).astype(BF16)
            chosen = jnp.dot(sel_bf, expand, preferred_element_type=F32)
            bias = jnp.where((chosen > 0.5) & (kpos <= tpos), 0.0, -BIG)[None]
            sc = lax.dot_general(qzb, kch.astype(BF16), NT_DIMS, preferred_element_type=F32
                                 ).reshape(NSA_HPG, Q_TILE, SEL_CHUNK) + bias
            m_new = jnp.maximum(m, jnp.max(sc, axis=-1, keepdims=True))
            alpha = jnp.exp(m - m_new)
            pp = jnp.exp(sc - m_new)
            l = alpha * l + jnp.sum(pp, axis=-1, keepdims=True)
            pv = _dot(pp.reshape(rows, SEL_CHUNK), vch).reshape(NSA_HPG, Q_TILE, LANES)
            return m_new, l, alpha * acc + pv

        init = (jnp.full((NSA_HPG, Q_TILE, 1), -BIG, F32), jnp.zeros((NSA_HPG, Q_TILE, 1), F32),
                jnp.zeros((NSA_HPG, Q_TILE, LANES), F32))
        _, l, acc = lax.fori_loop(0, n_chunks, sel_chunk, init)
        o_slc = (acc / l).reshape(rows, LANES)

        kw = win_ref[pl.ds(w0, win_span), 0:LANES]
        vw = win_ref[pl.ds(w0, win_span), LANES:2 * LANES]
        wpos = w0 + _iota((1, win_span), 1)
        w_bias = jnp.where((wpos <= tpos) & (wpos > tpos - NSA_WINDOW), 0.0, -BIG)[None]
        sw = lax.dot_general(qzb, kw.astype(BF16), NT_DIMS, preferred_element_type=F32
                             ).reshape(NSA_HPG, Q_TILE, win_span) + w_bias
        pw = jnp.exp(sw - jnp.max(sw, axis=-1, keepdims=True))
        o_win = (_dot(pw.reshape(rows, win_span), vw).reshape(NSA_HPG, Q_TILE, LANES)
                 / jnp.sum(pw, axis=-1, keepdims=True)).reshape(rows, LANES)

        o_cmp = jnp.concatenate([o_cmp_t[:, h * Q_TILE:(h + 1) * Q_TILE].T for h in range(NSA_HPG)], axis=0)
        branch_out.append([_place_heads(o, g, Q_TILE) for o in (o_cmp, o_slc, o_win)])

    for c in range(MIXER_WIDTH // LANES):
        g, cc = divmod(c, NSA_HPG // 2)
        sl = slice(c * LANES, (c + 1) * LANES)
        o_ref[:, sl] = sum(gexp[j][:, sl] * branch_out[g][j][cc] for j in range(3))


def nsa_attn_prompt(q, gates, cs, win, cmp, gexp, n_batch):
    n = q.shape[0]
    t = n // n_batch
    nt = t // Q_TILE
    nb = cmp.shape[0] // n_batch
    assert nb <= LANES
    cmp_t = cmp.reshape(n_batch, nb, 2 * LANES).transpose(0, 2, 1)
    return pl.pallas_call(
        _nsa_attn_kernel,
        grid=(n_batch, nt),
        in_specs=[pl.BlockSpec((Q_TILE, MIXER_WIDTH), lambda b, i: (b * nt + i, 0)),
                  pl.BlockSpec((Q_TILE, LANES), lambda b, i: (b * nt + i, 0)),
                  pl.BlockSpec((t, 4 * LANES), lambda b, i: (b, 0)),
                  pl.BlockSpec((t, 2 * LANES), lambda b, i: (b, 0)),
                  pl.BlockSpec((nb, 2 * LANES), lambda b, i: (b, 0)),
                  pl.BlockSpec((None, 2 * LANES, nb), lambda b, i: (b, 0, 0)),
                  pl.BlockSpec(gexp.shape, lambda b, i: (0, 0, 0))],
        out_specs=pl.BlockSpec((Q_TILE, MIXER_WIDTH), lambda b, i: (b * nt + i, 0)),
        out_shape=jax.ShapeDtypeStruct((n, MIXER_WIDTH), F32),
        scratch_shapes=[pltpu.VMEM((nb, Q_TILE), F32)],
        compiler_params=_params("parallel", "parallel"),
        name="nsa_attn_prompt",
    )(q, gates, cs, win, cmp, cmp_t, gexp)


SAMPLE_ROWS = 16
NB_PAD = 64


def _sample_queries(q_row):
    row = _iota((SAMPLE_ROWS, 1), 0)
    lane_group = _iota((1, LANES), 1) >> 6
    qz = jnp.zeros((SAMPLE_ROWS, LANES), F32)
    for hh in range(NSA_HEADS):
        g = hh // NSA_HPG
        chunk = jnp.broadcast_to(q_row[:, (hh // 2) * LANES:(hh // 2 + 1) * LANES], (SAMPLE_ROWS, LANES))
        if hh % 2 != g:
            chunk = pltpu.roll(chunk, HEAD_DIM, 1)
        qz = qz + jnp.where((row == hh) & (lane_group == g), chunk, 0.0)
    return qz * SCORE_SCALE


def _row_to_column(row):
    n = row.shape[1]
    diag = _iota((n, 1), 0) == _iota((1, n), 1)
    return jnp.sum(jnp.where(diag, jnp.broadcast_to(row, (n, n)), 0.0), axis=1, keepdims=True)


def _attend_with_new_key(s, ok, qz, k_new, v_new, weighted_values):
    s_new = jnp.sum(_bf16_round(qz) * _bf16_round(k_new), axis=-1, keepdims=True)
    m_lanes = jnp.max(jnp.where(ok, s, -BIG), axis=-1, keepdims=True)
    if s.ndim == 3:
        m_lanes = jnp.max(m_lanes, axis=0)
    m = jnp.maximum(m_lanes, s_new)
    p = jnp.where(ok, jnp.exp(s - m), 0.0)
    p_new = jnp.exp(s_new - m)
    total = jnp.sum(p, axis=-1, keepdims=True)
    if s.ndim == 3:
        total = jnp.sum(total, axis=0)
    den = total + p_new
    return (weighted_values(p) + _bf16_round(p_new) * _bf16_round(v_new)) / den


def _nsa_sample_kernel(pt_ref, q_ref, cmp_ref, new_ref, win_ref, wnew_ref, g_ref, gexp_ref, *refs, past):
    n_pages = past // PAGE_SIZE
    page_refs = refs[:n_pages]
    o_ref, wout_ref, cm_ref = refs[n_pages:]
    b = pl.program_id(0)
    head = _iota((SAMPLE_ROWS, 1), 0)
    qz = _sample_queries(q_ref[0])
    qzb = qz.astype(BF16)

    cm_ref[...] = jnp.zeros(cm_ref.shape, F32)
    for p in range(n_pages):
        row = cmp_ref[pl.ds(pt_ref[b * n_pages + p], 1), :]
        cm_ref[2 * p:2 * p + 1, :] = row[:, 0:2 * LANES]
        cm_ref[2 * p + 1:2 * p + 2, :] = row[:, 2 * LANES:4 * LANES]
    s = _dot_nt_precise(qz, cm_ref[:, 0:LANES])
    nidx = _iota((1, NB_PAD), 1)
    complete = (nidx + 1) * NSA_BLOCK <= past + 1
    p, den = _masked_softmax_parts(s, complete)
    p_cmp = p / den
    o_cmp = _dot(p_cmp, cm_ref[:, LANES:2 * LANES])

    row8 = _iota((SUBLANES, 1), 0)
    imp = jnp.zeros((SUBLANES, NB_PAD), F32)
    for g in range(NSA_GROUPS):
        in_group = (head >= g * NSA_HPG) & (head < (g + 1) * NSA_HPG)
        imp = imp + jnp.where(row8 == g, jnp.sum(jnp.where(in_group, p_cmp, 0.0), axis=0, keepdims=True), 0.0)
    cur = jnp.full((SUBLANES, 1), past // NSA_BLOCK, jnp.int32)
    sel = jnp.where(_select_blocks(imp, cur), 1.0, 0.0)
    sel_rows = jnp.where(head < NSA_HPG, sel[0:1], sel[1:2]).astype(BF16)
    kpos = _iota((1, past), 1)
    expand = jnp.where((kpos >> 6) == _iota((NB_PAD, 1), 0), 1.0, 0.0).astype(BF16)
    chosen = jnp.dot(sel_rows, expand, preferred_element_type=F32)

    s3 = jnp.concatenate(
        [jnp.dot(qzb, page_refs[p][0:LANES, :].astype(BF16), preferred_element_type=F32) for p in range(n_pages)],
        axis=0).reshape(n_pages, SAMPLE_ROWS, PAGE_SIZE)
    ok3 = jnp.concatenate([chosen[:, p * PAGE_SIZE:(p + 1) * PAGE_SIZE] for p in range(n_pages)],
                          axis=0).reshape(n_pages, SAMPLE_ROWS, PAGE_SIZE) > 0.5

    def sel_values(pp):
        return sum(_dot_nt(pp[p], page_refs[p][LANES:2 * LANES, :]) for p in range(n_pages))

    o_slc = _attend_with_new_key(s3, ok3, qz, new_ref[0][:, 2 * LANES:3 * LANES], new_ref[0][:, 3 * LANES:4 * LANES],
                                 sel_values)

    lw = win_ref.shape[1]
    sw = jnp.dot(qzb, win_ref[0:LANES, :].astype(BF16), preferred_element_type=F32)
    okw = _iota((1, lw), 1) > lw - NSA_WINDOW
    o_win = _attend_with_new_key(sw, okw, qz, wnew_ref[0][:, 0:LANES], wnew_ref[0][:, LANES:2 * LANES],
                                 lambda pp: _dot_nt(pp, win_ref[LANES:2 * LANES, :]))

    gates = jnp.broadcast_to(g_ref[0], (SUBLANES, LANES))
    lane_head = _iota((1, MIXER_WIDTH), 1) >> 6
    in_place = (head & 1) == jnp.where(head >= NSA_HPG, 1, 0)
    out = jnp.zeros((1, MIXER_WIDTH), F32)
    for j, o in enumerate((o_cmp, o_slc, o_win)):
        tiled = jnp.concatenate([o] * (MIXER_WIDTH // LANES), axis=1)
        rolled = jnp.concatenate([pltpu.roll(o, HEAD_DIM, 1)] * (MIXER_WIDTH // LANES), axis=1)
        placed = jnp.where(lane_head == head, jnp.where(in_place, tiled, rolled), 0.0)
        out = out + _dot_x2(gates, gexp_ref[j])[0:1] * jnp.sum(placed, axis=0, keepdims=True)
    o_ref[0] = out

    new_col = jnp.broadcast_to(_row_to_column(wnew_ref[0]), (2 * LANES, LANES))
    last_lane = _iota((1, LANES), 1) == LANES - 1
    for c in range(lw // LANES):
        shifted = pltpu.roll(win_ref[:, c * LANES:(c + 1) * LANES], LANES - 1, 1)
        if (c + 1) * LANES < lw:
            carry_in = pltpu.roll(win_ref[:, (c + 1) * LANES:(c + 2) * LANES], LANES - 1, 1)
        else:
            carry_in = new_col
        wout_ref[:, c * LANES:(c + 1) * LANES] = jnp.where(last_lane, carry_in, shifted)


def nsa_sample(page_table, layer, q, cmp_pool, cs_new, win_t, win_new, gates, gexp, pool_t):
    s, n_pages = page_table.shape
    past = n_pages * PAGE_SIZE
    lw = win_t.shape[-1]
    per_seq = lambda *shape: pl.BlockSpec((1,) + shape, lambda b, pt: (b,) + (0,) * len(shape))
    page_spec = lambda p: pl.BlockSpec((None, None, 2 * LANES, PAGE_SIZE),
                                       lambda b, pt: (layer, pt[b * n_pages + p], 1, 0))
    grid_spec = pltpu.PrefetchScalarGridSpec(
        num_scalar_prefetch=1,
        grid=(s,),
        in_specs=[per_seq(1, MIXER_WIDTH),
                  pl.BlockSpec(cmp_pool.shape, lambda b, pt: (0, 0)),
                  per_seq(1, 4 * LANES),
                  pl.BlockSpec((None, None, 2 * LANES, lw), lambda b, pt: (layer, b, 0, 0)),
                  per_seq(1, 2 * LANES), per_seq(1, LANES),
                  pl.BlockSpec(gexp.shape, lambda b, pt: (0, 0, 0))]
                 + [page_spec(p) for p in range(n_pages)],
        out_specs=[per_seq(1, MIXER_WIDTH), pl.BlockSpec((None, 2 * LANES, lw), lambda b, pt: (b, 0, 0))],
        scratch_shapes=[pltpu.VMEM((NB_PAD, 2 * LANES), F32)],
    )
    return pl.pallas_call(
        functools.partial(_nsa_sample_kernel, past=past),
        grid_spec=grid_spec,
        out_shape=[jax.ShapeDtypeStruct((s, 1, MIXER_WIDTH), F32),
                   jax.ShapeDtypeStruct((s, 2 * LANES, lw), F32)],
        compiler_params=_params("arbitrary"),
        name="nsa_sample",
    )(page_table.reshape(-1), q, cmp_pool, cs_new, win_t, win_new, gates, gexp, *([pool_t] * n_pages))


def _layer_norm_swish(y, g, b):
    mu = jnp.mean(y, axis=-1, keepdims=True)
    var = jnp.mean(jnp.square(y - mu), axis=-1, keepdims=True)
    z = (y - mu) * lax.rsqrt(var + EPS) * g + b
    return z * _sigmoid(z)


def _conv_prompt_kernel(a_ref, b_ref, ap_ref, bp_ref, w_ref, cb_ref, lg_ref, lb_ref, o_ref, st_ref, buf_ref):
    i = pl.program_id(1)
    glu = a_ref[...] * _sigmoid(b_ref[...])
    prev = ap_ref[...] * _sigmoid(bp_ref[...])
    buf_ref[0:CONV_HALO, :] = jnp.where(i > 0, prev, 0.0)
    buf_ref[CONV_HALO:, :] = glu
    tt = a_ref.shape[0]
    acc = jnp.zeros(glu.shape, F32)
    for k in range(CONV_WIDTH):
        acc = acc + w_ref[k:k + 1, :] * buf_ref[pl.ds(CONV_HALO - (CONV_WIDTH - 1) + k, tt), :]
    o_ref[...] = _layer_norm_swish(acc + cb_ref[...], lg_ref[...], lb_ref[...])

    @pl.when(i == pl.num_programs(1) - 1)
    def _():
        st_ref[0] = glu[tt - CONV_HALO:, :]


def conv_prompt(u, w, cb, lg, lb, n_batch):
    n = u.shape[0]
    t = n // n_batch
    tt = min(CONV_TILE, t)
    nt = t // tt
    ch = MIXER_WIDTH
    ratio = tt // CONV_HALO
    cur = lambda col: pl.BlockSpec((tt, ch), lambda b, i: (b * nt + i, col))
    prev = lambda col: pl.BlockSpec((CONV_HALO, ch), lambda b, i: (jnp.maximum((b * nt + i) * ratio - 1, 0), col))
    const = lambda *shape: pl.BlockSpec(shape, lambda b, i: (0,) * len(shape))
    return pl.pallas_call(
        _conv_prompt_kernel,
        grid=(n_batch, nt),
        in_specs=[cur(0), cur(1), prev(0), prev(1), const(CONV_HALO, ch), const(1, ch), const(1, ch), const(1, ch)],
        out_specs=[pl.BlockSpec((tt, ch), lambda b, i: (b * nt + i, 0)),
                   pl.BlockSpec((1, CONV_HALO, ch), lambda b, i: (b, 0, 0))],
        out_shape=[jax.ShapeDtypeStruct((n, ch), F32),
                   jax.ShapeDtypeStruct((n_batch, CONV_HALO, ch), F32)],
        scratch_shapes=[pltpu.VMEM((CONV_HALO + tt, ch), F32)],
        compiler_params=_params("parallel", "arbitrary"),
        name="conv_prompt",
    )(u, u, u, u, w, cb, lg, lb)


def _conv_sample_kernel(a_ref, b_ref, st_ref, w_ref, cb_ref, lg_ref, lb_ref, o_ref, glu_ref):
    glu = a_ref[...] * _sigmoid(b_ref[...])
    acc = w_ref[CONV_WIDTH - 1:CONV_WIDTH, :] * glu
    for k in range(CONV_WIDTH - 1):
        acc = acc + w_ref[k:k + 1, :] * st_ref[k]
    o_ref[...] = _layer_norm_swish(acc + cb_ref[...], lg_ref[...], lb_ref[...])
    glu_ref[...] = glu


def conv_sample(u, state_t, layer, w, cb, lg, lb):
    s = u.shape[0]
    ch = MIXER_WIDTH
    const = lambda *shape: pl.BlockSpec(shape, lambda i: (0,) * len(shape))
    return pl.pallas_call(
        _conv_sample_kernel,
        grid=(1,),
        in_specs=[pl.BlockSpec((s, ch), lambda i: (0, 0)), pl.BlockSpec((s, ch), lambda i: (0, 1)),
                  pl.BlockSpec((None, CONV_WIDTH - 1, s, ch), lambda i: (layer, 0, 0, 0)),
                  const(CONV_HALO, ch), const(1, ch), const(1, ch), const(1, ch)],
        out_specs=[const(s, ch), const(s, ch)],
        out_shape=[jax.ShapeDtypeStruct((s, ch), F32), jax.ShapeDtypeStruct((s, ch), F32)],
        compiler_params=_params("arbitrary"),
        name="conv_sample",
    )(u, u, state_t, w, cb, lg, lb)


SB_PAIRS = MIXER_WIDTH // LANES


def _stick_weights(z, valid, later, tri):
    sp = _softplus(z)
    log_stay = jnp.where(valid, -sp, 0.0)
    within = _dot_x2(log_stay, tri)
    a = jnp.where(valid, jnp.exp(z - sp + within + later), 0.0)
    return a, within[:, 0:1] + log_stay[:, 0:1]


def _sb_attn_kernel(q_ref, k_ref, v_ref, tri_ref, o_ref, acc_ref):
    qt = pl.program_id(1)
    t0 = qt * Q_TILE
    low = _iota((1, LANES), 1) < HEAD_DIM
    pair_rows = 2 * Q_TILE
    tpos = t0 + (_iota((SB_PAIRS * pair_rows, 1), 0) & (Q_TILE - 1))
    tri = tri_ref[...]
    qz = []
    for pr in range(SB_PAIRS):
        q2 = q_ref[:, pr * LANES:(pr + 1) * LANES] * SCORE_SCALE
        qz.append(jnp.concatenate([jnp.where(low, q2, 0.0), jnp.where(low, 0.0, q2)], axis=0).astype(BF16))
    acc_ref[...] = jnp.zeros(acc_ref.shape, F32)

    def alive(carry):
        return (carry[0] <= qt) & (carry[1] > 0)

    def chunk(carry):
        i, _, later = carry
        k0 = pl.multiple_of((qt - i) * SB_CHUNK, SB_CHUNK)
        valid = (k0 + _iota((1, SB_CHUNK), 1)) < tpos
        z = jnp.concatenate(
            [lax.dot_general(qz[pr], k_ref[pl.ds(k0, SB_CHUNK), pr * LANES:(pr + 1) * LANES], NT_DIMS,
                             preferred_element_type=F32) for pr in range(SB_PAIRS)], axis=0)
        a, total = _stick_weights(z, valid, later, tri)
        a = a.astype(BF16)
        for pr in range(SB_PAIRS):
            acc_ref[pr] += jnp.dot(a[pr * pair_rows:(pr + 1) * pair_rows],
                                   v_ref[pl.ds(k0, SB_CHUNK), pr * LANES:(pr + 1) * LANES],
                                   preferred_element_type=F32)
        later = later + total
        live = jnp.max(later) > UNDERFLOW_LOG
        return i + 1, live.astype(jnp.int32), later

    init = (jnp.int32(0), jnp.int32(1), jnp.zeros((SB_PAIRS * pair_rows, 1), F32))
    lax.while_loop(alive, chunk, init)
    for pr in range(SB_PAIRS):
        o_ref[:, pr * LANES:(pr + 1) * LANES] = jnp.where(low, acc_ref[pr, :Q_TILE], acc_ref[pr, Q_TILE:])


def sb_attn_prompt(u, kv16, tri, n_batch):
    n = u.shape[0]
    t = n // n_batch
    nt = t // Q_TILE
    return pl.pallas_call(
        _sb_attn_kernel,
        grid=(n_batch, nt),
        in_specs=[pl.BlockSpec((Q_TILE, MIXER_WIDTH), lambda b, i: (b * nt + i, 0)),
                  pl.BlockSpec((t, MIXER_WIDTH), lambda b, i: (b, 0)),
                  pl.BlockSpec((t, MIXER_WIDTH), lambda b, i: (b, 1)),
                  pl.BlockSpec(tri.shape, lambda b, i: (0, 0))],
        out_specs=pl.BlockSpec((Q_TILE, MIXER_WIDTH), lambda b, i: (b * nt + i, 0)),
        out_shape=jax.ShapeDtypeStruct((n, MIXER_WIDTH), F32),
        scratch_shapes=[pltpu.VMEM((SB_PAIRS, 2 * Q_TILE, LANES), F32)],
        compiler_params=_params("parallel", "parallel"),
        name="sb_attn_prompt",
    )(u, kv16, kv16, tri)


def _sb_sample_kernel(pt_ref, q_ref, tri_ref, suffix_ref, *refs):
    page_refs, o_ref = refs[:-1], refs[-1]
    n_pages = len(page_refs)
    head = _iota((SAMPLE_ROWS, 1), 0)
    lane_head = _iota((1, MIXER_WIDTH), 1) >> 6
    qz = (jnp.where(lane_head == head, jnp.broadcast_to(q_ref[0], (SAMPLE_ROWS, MIXER_WIDTH)), 0.0)
          * SCORE_SCALE).astype(BF16)
    z = jnp.concatenate(
        [jnp.dot(qz, page_refs[p][0:MIXER_WIDTH, :].astype(BF16), preferred_element_type=F32) for p in range(n_pages)],
        axis=0)
    sp = _softplus(z)
    log_stay = -sp
    within = _dot_x3(log_stay, tri_ref[...])
    total = jnp.broadcast_to(within[:, 0:1] + log_stay[:, 0:1], z.shape)
    later = _dot_w3(suffix_ref[...], total)
    a = jnp.exp(z - sp + within + later)
    acc = sum(_dot_nt(a[p * SAMPLE_ROWS:(p + 1) * SAMPLE_ROWS], page_refs[p][MIXER_WIDTH:2 * MIXER_WIDTH, :])
              for p in range(n_pages))
    o_ref[0] = jnp.sum(jnp.where(lane_head == head, acc, 0.0), axis=0, keepdims=True)


def sb_attn_sample(page_table, layer, q, pool_t, tri, suffix):
    s, n_pages = page_table.shape
    page_spec = lambda p: pl.BlockSpec((None, None, 2 * MIXER_WIDTH, PAGE_SIZE),
                                       lambda b, pt: (layer, pt[b * n_pages + p], 0, 0))
    grid_spec = pltpu.PrefetchScalarGridSpec(
        num_scalar_prefetch=1,
        grid=(s,),
        in_specs=[pl.BlockSpec((1, 1, MIXER_WIDTH), lambda b, pt: (b, 0, 0)),
                  pl.BlockSpec(tri.shape, lambda b, pt: (0, 0)),
                  pl.BlockSpec(suffix.shape, lambda b, pt: (0, 0))]
                 + [page_spec(p) for p in range(n_pages)],
        out_specs=pl.BlockSpec((1, 1, MIXER_WIDTH), lambda b, pt: (b, 0, 0)),
    )
    return pl.pallas_call(
        _sb_sample_kernel,
        grid_spec=grid_spec,
        out_shape=jax.ShapeDtypeStruct((s, 1, MIXER_WIDTH), F32),
        compiler_params=_params("arbitrary"),
        name="sb_attn_sample",
    )(page_table.reshape(-1), q, tri, suffix, *([pool_t] * n_pages))


def _rope_tables(positions):
    half = ROT_DIM // 2
    inv = np.exp(-math.log(ROPE_THETA) * np.arange(0, ROT_DIM, 2, dtype=np.float64) / ROT_DIM)
    ang = np.asarray(positions, np.float64)[:, None] * inv[None, :]
    lane = np.arange(LANES) % HEAD_DIM
    cos = np.ones((len(positions), LANES))
    sa = np.zeros((len(positions), LANES))
    sb = np.zeros((len(positions), LANES))
    for l in range(LANES):
        r = lane[l]
        if r < half:
            cos[:, l] = np.cos(ang[:, r])
            sb[:, l] = -np.sin(ang[:, r])
        elif r < ROT_DIM:
            cos[:, l] = np.cos(ang[:, r - half])
            sa[:, l] = np.sin(ang[:, r - half])
    return tuple(jnp.asarray(a, F32) for a in (cos, sa, sb))


def _segment_ones():
    lane = np.arange(LANES)
    return jnp.asarray(lane[:, None] // HEAD_DIM == lane[None, :] // HEAD_DIM, BF16)


def _gate_expand():
    e = np.zeros((3, LANES, MIXER_WIDTH), np.float32)
    for j in range(3):
        for h in range(NSA_HEADS):
            e[j, 3 * h + j, h * HEAD_DIM:(h + 1) * HEAD_DIM] = 1.0
    return jnp.asarray(e, BF16)


def _later_ones(n):
    idx = np.arange(n)
    return jnp.asarray(idx[:, None] > idx[None, :], BF16)


def _later_pages(n_pages):
    p = np.repeat(np.arange(n_pages), SAMPLE_ROWS)
    r = np.tile(np.arange(SAMPLE_ROWS), n_pages)
    return jnp.asarray((r[:, None] == r[None, :]) & (p[None, :] > p[:, None]), BF16)


def _compress_weights(w_cmp):
    w = w_cmp.reshape(2, NSA_BLOCK, HEAD_DIM, HEAD_DIM)
    return (jnp.tile(w, (1, 1, NSA_GROUPS, NSA_GROUPS)) * _segment_ones()).astype(BF16)


def _compress_weights_t(w_cmp):
    assert NSA_BLOCK == HEAD_DIM and PAGE_SIZE == 2 * NSA_BLOCK
    w = w_cmp.reshape(2, NSA_BLOCK, HEAD_DIM, HEAD_DIM).transpose(0, 2, 1, 3)
    return (jnp.tile(w, (1, 1, 2, 2)) * _segment_ones()).astype(BF16)


def _rows_last(x):
    a, b, rows = x.shape[:3]
    perm = (0, 1) + tuple(range(3, x.ndim)) + (2,)
    return x.transpose(perm).reshape(a, b, -1, rows)


def kernel(x_prompt, x_sample, cache_nsa_kv, state_nsa_win, state_conv, cache_sb_kv, cache_mem_kv, page_table,
           mem_prompt, norm_mix, norm_ffn, norm_mem, w_in_nsa, w_in_conv, w_in_sb, w_out, w_mem_kv, qk_norm_nsa,
           qk_norm_mem, w_nsa_cmp, conv_w, conv_b, conv_ln_g, conv_ln_b, w_ffn_in, w_ffn_out):
    n_batch, seq, d_model = x_prompt.shape
    n_dec = x_sample.shape[0]
    depth = w_out.shape[0]
    n_pages = page_table.shape[1]
    past = n_pages * PAGE_SIZE
    n_mem = mem_prompt.shape[1]
    ffn_hidden = w_ffn_out.shape[1]
    assert x_sample.shape[1] == 1 and seq % CONV_TILE == 0 and seq >= NSA_WINDOW + Q_TILE
    assert past % NSA_BLOCK == 0 and ffn_hidden % LANES == 0 and n_dec % MEM_BATCH == 0
    assert PAGE_SIZE == SB_CHUNK == LANES and state_nsa_win.shape[2] % LANES == 0

    xp = x_prompt.reshape(n_batch * seq, d_model)
    xs = x_sample.reshape(n_dec, d_model)

    seg = _segment_ones()
    gexp = _gate_expand()
    tri = _later_ones(SB_CHUNK)
    suffix = _later_pages(n_pages)
    rope_p = _rope_tables(np.arange(seq))
    rope_s = _rope_tables(np.full((n_dec,), past))

    nsa_pool_t = _rows_last(cache_nsa_kv)
    nsa_win_t = _rows_last(state_nsa_win)
    sb_pool_t = _rows_last(cache_sb_kv)
    mem_cache_t = _rows_last(cache_mem_kv)
    conv_state_t = state_conv.transpose(0, 2, 1, 3)

    nq, nkv = MIXER_WIDTH, 6 * NSA_GROUPS * HEAD_DIM
    n_gate = 3 * NSA_HEADS
    w_nsa = jnp.concatenate(
        [w_in_nsa[:, :, :nq + nkv], w_in_nsa[:, :, nq + nkv + n_gate:], w_in_nsa[:, :, nq + nkv:nq + nkv + n_gate],
         jnp.zeros(w_in_nsa.shape[:2] + (LANES - n_gate,), F32)], axis=-1).astype(BF16)
    w_conv = w_in_conv.astype(BF16)
    w_sb = w_in_sb.astype(BF16)
    w_o1 = w_out[:, :MIXER_WIDTH].astype(BF16)
    w_o2 = w_out[:, MIXER_WIDTH:].astype(BF16)
    w_gate_up = w_ffn_in.astype(BF16)
    w_down = w_ffn_out.astype(BF16)

    gq_mem = jnp.tile(qk_norm_mem[:, 0], (1, LANES // HEAD_DIM))
    gk_mem = jnp.tile(qk_norm_mem[:, 1], (1, LANES // HEAD_DIM))
    mem_kv = mem_kv_all(mem_prompt.reshape(n_batch * n_mem, d_model), norm_mem, w_mem_kv.astype(BF16), gk_mem, seg)

    nsa_p, nsa_s, win_p, win_s, conv_p, conv_s, sb_p, sb_s = [], [], [], [], [], [], [], []
    for layer in range(depth):
        kind, j = layer % N_MIXERS, layer // N_MIXERS
        if kind == 0:
            up = proj_in(xp, norm_mix[layer], w_nsa[j])
            us = proj_in(xs, norm_mix[layer], w_nsa[j])
            mem_col = 2 * MIXER_WIDTH // MEM_WIDTH
            gq = jnp.tile(qk_norm_nsa[j, 0], LANES // HEAD_DIM).reshape(1, LANES)
            gkv = jnp.tile(qk_norm_nsa[j, 1:4], (1, LANES // HEAD_DIM)).reshape(3, 1, LANES)

            q_p, cs_p, wn_p, gt_p, cs_pt, wn_pt = nsa_prep(up, rope_p, gq, gkv, seg, seq // 256, 256, rows_last=True)
            cmp_p = compress(cs_p, _compress_weights(w_nsa_cmp[j]))
            o_p = nsa_attn_prompt(q_p, gt_p, cs_p, wn_p, cmp_p, gexp, n_batch)

            q_s, cs_s, wn_s, gt_s = nsa_prep(us, rope_s, gq, gkv, seg, 1, n_dec)
            cmp_pool = compress_pool(nsa_pool_t, j, _compress_weights_t(w_nsa_cmp[j]))
            o_s, win_new_t = nsa_sample(page_table, j, q_s.reshape(n_dec, 1, MIXER_WIDTH), cmp_pool,
                                        cs_s.reshape(n_dec, 1, 4 * LANES), nsa_win_t,
                                        wn_s.reshape(n_dec, 1, 2 * LANES), gt_s.reshape(n_dec, 1, LANES), gexp,
                                        nsa_pool_t)
            o_s = o_s.reshape(n_dec, MIXER_WIDTH)

            lw = nsa_win_t.shape[-1]
            keep = min(NSA_WINDOW, seq)
            nsa_p.append(cs_pt.reshape(n_batch, 4, NSA_GROUPS, HEAD_DIM, seq).transpose(0, 4, 1, 2, 3))
            nsa_s.append(cs_s.reshape(n_dec, 1, 4, NSA_GROUPS, HEAD_DIM))
            win_p.append(wn_pt[:, :, seq - keep:].reshape(n_batch, 2, NSA_GROUPS, HEAD_DIM, keep).transpose(0, 4, 1, 2, 3))
            win_s.append(win_new_t.reshape(n_dec, 2, NSA_GROUPS, HEAD_DIM, lw).transpose(0, 4, 1, 2, 3))
        elif kind == 1:
            up = proj_in(xp, norm_mix[layer], w_conv[j])
            us = proj_in(xs, norm_mix[layer], w_conv[j])
            mem_col = 2 * MIXER_WIDTH // MEM_WIDTH
            cw = jnp.concatenate([conv_w[j], jnp.zeros((CONV_HALO - CONV_WIDTH, MIXER_WIDTH), F32)], axis=0)
            vecs = [v[j].reshape(1, MIXER_WIDTH) for v in (conv_b, conv_ln_g, conv_ln_b)]
            o_p, tail = conv_prompt(up, cw, *vecs, n_batch)
            o_s, glu_s = conv_sample(us, conv_state_t, j, cw, *vecs)
            conv_p.append(tail[:, CONV_HALO - (CONV_WIDTH - 1):])
            conv_s.append(jnp.concatenate([conv_state_t[j, 1:], glu_s[None]], axis=0).transpose(1, 0, 2))
        else:
            up, kv16, kv_t = proj_in(xp, norm_mix[layer], w_sb[j], narrow_cols=(MIXER_WIDTH, 3 * MIXER_WIDTH),
                                     n_batch=n_batch)
            us = proj_in(xs, norm_mix[layer], w_sb[j])
            mem_col = 3 * MIXER_WIDTH // MEM_WIDTH
            o_p = sb_attn_prompt(up, kv16, tri, n_batch)
            o_s = sb_attn_sample(page_table, j, us[:, :MIXER_WIDTH].reshape(n_dec, 1, MIXER_WIDTH), sb_pool_t, tri,
                                 suffix).reshape(n_dec, MIXER_WIDTH)
            sb_p.append(kv_t.reshape(n_batch, 2, NSA_HEADS, HEAD_DIM, seq).transpose(0, 4, 1, 2, 3))
            sb_s.append(us[:, MIXER_WIDTH:3 * MIXER_WIDTH].reshape(n_dec, 1, 2, NSA_HEADS, HEAD_DIM))

        gq_l = gq_mem[layer].reshape(1, LANES)
        om_p = mem_attn_prompt(up, mem_col, mem_kv[layer], gq_l, seg, n_batch)
        om_s = mem_attn_sample(us[:, mem_col * MEM_WIDTH:(mem_col + 1) * MEM_WIDTH], mem_cache_t, layer, gq_l, seg)

        ffn_w = (w_o1[layer], w_o2[layer], norm_ffn[layer], w_gate_up[layer], w_down[layer])
        xp = out_ffn(o_p, om_p, xp, *ffn_w)
        xs = out_ffn(o_s, om_s, xs, *ffn_w)

    return (xp.reshape(x_prompt.shape), xs.reshape(x_sample.shape),
            jnp.stack(nsa_p), jnp.stack(nsa_s), jnp.stack(win_p), jnp.stack(win_s),
            jnp.stack(conv_p), jnp.stack(conv_s), jnp.stack(sb_p), jnp.stack(sb_s),
            mem_kv.reshape(depth, n_batch, n_mem, 2, MEM_HEADS, HEAD_DIM))
```

```python
import functools
import math

import numpy as np
import jax
import jax.numpy as jnp
from jax import lax
from jax.experimental import pallas as pl
from jax.experimental.pallas import tpu as pltpu

F32 = jnp.float32
BF16 = jnp.bfloat16

HEAD_DIM = 64
MEM_HEADS = 4
MEM_WIDTH = MEM_HEADS * HEAD_DIM
N_MIXERS = 3
NSA_HEADS = 12
NSA_GROUPS = 2
NSA_HPG = NSA_HEADS // NSA_GROUPS
NSA_BLOCK = 64
NSA_TOPK = 16
NSA_WINDOW = 512
MIXER_WIDTH = NSA_HEADS * HEAD_DIM
CONV_WIDTH = 31
ROPE_THETA = 500000.0
ROT_DIM = HEAD_DIM // 4
EPS = 1e-6
PAGE_SIZE = 128

LANES = 128
SUBLANES = 8
VMEM_LIMIT = 56 * 1024 * 1024

ROW_TILE = 512
FFN_TILE = 256
Q_TILE = 128
SEL_CHUNK = 512
SB_CHUNK = 128
CONV_TILE = 512
CMP_BLOCKS = 256
CMP_PAGES = 128
MEM_BATCH = 8
CONV_HALO = 32

SCORE_SCALE = HEAD_DIM ** -0.5
BIG = 1e30
UNDERFLOW_LOG = -104.0
NT_DIMS = (((1,), (1,)), ((), ()))


def _params(*sem):
    return pltpu.CompilerParams(dimension_semantics=sem, vmem_limit_bytes=VMEM_LIMIT)


def _pick_tile(n, cap):
    return max(t for t in range(SUBLANES, cap + 1, SUBLANES) if n % t == 0)


def _iota(shape, dim):
    return lax.broadcasted_iota(jnp.int32, shape, dim)


def _dot(a, b):
    return jnp.dot(a.astype(BF16), b.astype(BF16), preferred_element_type=F32)


def _dot_nt(a, b):
    return lax.dot_general(a.astype(BF16), b.astype(BF16), NT_DIMS, preferred_element_type=F32)


def _split2(x):
    hi = x.astype(BF16)
    lo = (x - hi.astype(F32)).astype(BF16)
    return hi, lo


def _split3(x):
    hi = x.astype(BF16)
    r = x - hi.astype(F32)
    mid = r.astype(BF16)
    lo = (r - mid.astype(F32)).astype(BF16)
    return hi, mid, lo


def _dot_x2(x, w_bf16):
    return sum(jnp.dot(part, w_bf16, preferred_element_type=F32) for part in _split2(x))


def _dot_x3(x, w_bf16):
    return sum(jnp.dot(part, w_bf16, preferred_element_type=F32) for part in _split3(x))


def _dot_w3(w_bf16, x):
    return sum(jnp.dot(w_bf16, part, preferred_element_type=F32) for part in _split3(x))


def _dot_nt_precise(a, b):
    ah, al = _split2(a)
    bh, bl = _split2(b)
    return (lax.dot_general(ah, bh, NT_DIMS, preferred_element_type=F32)
            + lax.dot_general(ah, bl, NT_DIMS, preferred_element_type=F32)
            + lax.dot_general(al, bh, NT_DIMS, preferred_element_type=F32))


def _bf16_round(x):
    return x.astype(BF16).astype(F32)


def _rms_rows(x, g):
    ms = jnp.mean(x * x, axis=-1, keepdims=True)
    return x * lax.rsqrt(ms + EPS) * g


def _head_rms(x, seg_ones, g):
    ms = _dot_x2(x * x, seg_ones) * (1.0 / HEAD_DIM)
    return x * lax.rsqrt(ms + EPS) * g


def _sigmoid(x):
    return 1.0 / (1.0 + jnp.exp(-x))


def _softplus(z):
    return jnp.maximum(z, 0.0) + jnp.log1p(jnp.exp(-jnp.abs(z)))


def _proj_in_kernel(x_ref, g_ref, w_ref, o_ref, *extra_refs, narrow_cols):
    h = _rms_rows(x_ref[...], g_ref[...]).astype(BF16)
    u = jnp.dot(h, w_ref[...], preferred_element_type=F32)
    o_ref[...] = u
    if narrow_cols is not None:
        narrow_ref, rows_last_ref = extra_refs
        narrow_ref[...] = u[:, narrow_cols[0]:narrow_cols[1]].astype(BF16)
        for c in range((narrow_cols[1] - narrow_cols[0]) // LANES):
            lo = narrow_cols[0] + c * LANES
            rows_last_ref[c * LANES:(c + 1) * LANES, :] = u[:, lo:lo + LANES].T


def proj_in(x, g, w, narrow_cols=None, n_batch=None):
    n, d = x.shape
    m = w.shape[1]
    tm = min(ROW_TILE, n)
    out_specs = [pl.BlockSpec((tm, m), lambda i: (i, 0))]
    out_shape = [jax.ShapeDtypeStruct((n, m), F32)]
    if narrow_cols is not None:
        width = narrow_cols[1] - narrow_cols[0]
        seq = n // n_batch
        nt = seq // tm
        out_specs.append(pl.BlockSpec((tm, width), lambda i: (i, 0)))
        out_shape.append(jax.ShapeDtypeStruct((n, width), BF16))
        out_specs.append(pl.BlockSpec((None, width, tm), lambda i: (i // nt, 0, i % nt)))
        out_shape.append(jax.ShapeDtypeStruct((n_batch, width, seq), F32))
    res = pl.pallas_call(
        functools.partial(_proj_in_kernel, narrow_cols=narrow_cols),
        grid=(n // tm,),
        in_specs=[pl.BlockSpec((tm, d), lambda i: (i, 0)),
                  pl.BlockSpec((1, d), lambda i: (0, 0)),
                  pl.BlockSpec((d, m), lambda i: (0, 0))],
        out_specs=out_specs,
        out_shape=out_shape,
        compiler_params=_params("parallel"),
        name="proj_in",
    )(x, g.reshape(1, d), w)
    return res if narrow_cols is not None else res[0]


def _out_ffn_kernel(o_ref, om_ref, x_ref, w1_ref, w2_ref, gf_ref, wgu_ref, wo_ref, y_ref):
    x = x_ref[...] + _dot(o_ref[...], w1_ref[...]) + _dot(om_ref[...], w2_ref[...])
    h = _rms_rows(x, gf_ref[...]).astype(BF16)
    hidden = wo_ref.shape[0]
    gate_up = jnp.dot(h, wgu_ref[...], preferred_element_type=F32)
    gate, up = gate_up[:, :hidden], gate_up[:, hidden:]
    act = (gate * _sigmoid(gate) * up).astype(BF16)
    y_ref[...] = x + jnp.dot(act, wo_ref[...], preferred_element_type=F32)


def out_ffn(o, om, x, w1, w2, gf, wgu, wo):
    n, d = x.shape
    tm = min(FFN_TILE, n)
    const = lambda *shape: pl.BlockSpec(shape, lambda i: (0,) * len(shape), pipeline_mode=pl.Buffered(1))
    return pl.pallas_call(
        _out_ffn_kernel,
        grid=(n // tm,),
        in_specs=[pl.BlockSpec((tm, o.shape[1]), lambda i: (i, 0)),
                  pl.BlockSpec((tm, om.shape[1]), lambda i: (i, 0)),
                  pl.BlockSpec((tm, d), lambda i: (i, 0)),
                  const(*w1.shape), const(*w2.shape), const(1, d), const(*wgu.shape), const(*wo.shape)],
        out_specs=pl.BlockSpec((tm, d), lambda i: (i, 0)),
        out_shape=jax.ShapeDtypeStruct((n, d), F32),
        compiler_params=_params("parallel"),
        name="out_ffn",
    )(o, om, x, w1, w2, gf.reshape(1, d), wgu, wo)


def _mem_kv_kernel(mem_ref, gn_ref, w_ref, gk_ref, seg_ref, o_ref):
    h = _rms_rows(mem_ref[...], gn_ref[0]).astype(BF16)
    kv = jnp.dot(h, w_ref[0], preferred_element_type=F32)
    seg = seg_ref[...]
    for c in range(MEM_WIDTH // LANES):
        sl = slice(c * LANES, (c + 1) * LANES)
        o_ref[0, :, sl] = _head_rms(kv[:, sl], seg, gk_ref[0])
    o_ref[0, :, MEM_WIDTH:] = kv[:, MEM_WIDTH:]


def mem_kv_all(mem, g_norm, w_kv, g_k, seg):
    n, d = mem.shape
    nl = w_kv.shape[0]
    return pl.pallas_call(
        _mem_kv_kernel,
        grid=(nl,),
        in_specs=[pl.BlockSpec((n, d), lambda l: (0, 0)),
                  pl.BlockSpec((1, 1, d), lambda l: (l, 0, 0)),
                  pl.BlockSpec((1, d, 2 * MEM_WIDTH), lambda l: (l, 0, 0)),
                  pl.BlockSpec((1, 1, LANES), lambda l: (l, 0, 0)),
                  pl.BlockSpec((LANES, LANES), lambda l: (0, 0))],
        out_specs=pl.BlockSpec((1, n, 2 * MEM_WIDTH), lambda l: (l, 0, 0)),
        out_shape=jax.ShapeDtypeStruct((nl, n, 2 * MEM_WIDTH), F32),
        compiler_params=_params("arbitrary"),
        name="mem_kv",
    )(mem, g_norm.reshape(nl, 1, d), w_kv, g_k.reshape(nl, 1, LANES), seg)


def _mem_query(qm, seg, gq):
    parts = [_head_rms(qm[:, c * LANES:(c + 1) * LANES], seg, gq) for c in range(MEM_WIDTH // LANES)]
    return jnp.concatenate(parts, axis=1) * SCORE_SCALE


def _softmax_rows(s):
    p = jnp.exp(s - jnp.max(s, axis=-1, keepdims=True))
    return p / jnp.sum(p, axis=-1, keepdims=True)


def _mem_attn_kernel(qm_ref, kv_ref, gq_ref, seg_ref, o_ref):
    qn = _mem_query(qm_ref[...], seg_ref[...], gq_ref[...])
    r = qn.shape[0]
    head_of_lane = _iota((1, MEM_WIDTH), 1) >> 6
    qz = jnp.concatenate([jnp.where(head_of_lane == h, qn, 0.0) for h in range(MEM_HEADS)], axis=0)
    p = _softmax_rows(_dot_nt(qz, kv_ref[:, :MEM_WIDTH]))
    res = _dot(p, kv_ref[:, MEM_WIDTH:])
    out = jnp.zeros((r, MEM_WIDTH), F32)
    for h in range(MEM_HEADS):
        out = out + jnp.where(head_of_lane == h, res[h * r:(h + 1) * r], 0.0)
    o_ref[...] = out


def mem_attn_prompt(u, mem_col, kv, gq, seg, n_batch):
    n = u.shape[0]
    t = n // n_batch
    tm = 256
    n_mem = kv.shape[0] // n_batch
    nt = t // tm
    return pl.pallas_call(
        _mem_attn_kernel,
        grid=(n_batch, nt),
        in_specs=[pl.BlockSpec((tm, MEM_WIDTH), lambda b, i: (b * nt + i, mem_col)),
                  pl.BlockSpec((n_mem, 2 * MEM_WIDTH), lambda b, i: (b, 0)),
                  pl.BlockSpec((1, LANES), lambda b, i: (0, 0)),
                  pl.BlockSpec((LANES, LANES), lambda b, i: (0, 0))],
        out_specs=pl.BlockSpec((tm, MEM_WIDTH), lambda b, i: (b * nt + i, 0)),
        out_shape=jax.ShapeDtypeStruct((n, MEM_WIDTH), F32),
        compiler_params=_params("parallel", "parallel"),
        name="mem_attn_prompt",
    )(u, kv, gq, seg)


def _mem_attn_sample_kernel(qm_ref, kv_ref, gq_ref, seg_ref, o_ref):
    qn = _mem_query(qm_ref[0], seg_ref[...], gq_ref[...])
    row = _iota((SUBLANES, 1), 0)
    head_of_lane = _iota((1, MEM_WIDTH), 1) >> 6
    out = jnp.zeros((MEM_BATCH, MEM_WIDTH), F32)
    for i in range(MEM_BATCH):
        qz = jnp.where(head_of_lane == (row & (MEM_HEADS - 1)), jnp.broadcast_to(qn[i:i + 1], (SUBLANES, MEM_WIDTH)), 0.0)
        p = _softmax_rows(_dot(qz, kv_ref[i, 0:MEM_WIDTH, :]))
        res = _dot_nt(p, kv_ref[i, MEM_WIDTH:2 * MEM_WIDTH, :])
        flat = jnp.sum(jnp.where(head_of_lane == row, res, 0.0), axis=0, keepdims=True)
        out = out + jnp.where(_iota((MEM_BATCH, 1), 0) == i, flat, 0.0)
    o_ref[0] = out


def mem_attn_sample(qm, kv_t, layer, gq, seg):
    _, s, _, n_mem = kv_t.shape
    nb = s // MEM_BATCH
    return pl.pallas_call(
        _mem_attn_sample_kernel,
        grid=(nb,),
        in_specs=[pl.BlockSpec((1, MEM_BATCH, MEM_WIDTH), lambda b: (b, 0, 0)),
                  pl.BlockSpec((None, MEM_BATCH, 2 * MEM_WIDTH, n_mem), lambda b: (layer, b, 0, 0)),
                  pl.BlockSpec((1, LANES), lambda b: (0, 0)),
                  pl.BlockSpec((LANES, LANES), lambda b: (0, 0))],
        out_specs=pl.BlockSpec((1, MEM_BATCH, MEM_WIDTH), lambda b: (b, 0, 0)),
        out_shape=jax.ShapeDtypeStruct((nb, MEM_BATCH, MEM_WIDTH), F32),
        compiler_params=_params("parallel"),
        name="mem_attn_sample",
    )(qm.reshape(nb, MEM_BATCH, MEM_WIDTH), kv_t, gq, seg).reshape(s, MEM_WIDTH)


def _nsa_prep_kernel(q_ref, kv_ref, gt_ref, cos_ref, sa_ref, sb_ref, gq_ref, gkv_ref, seg_ref,
                     qo_ref, cso_ref, wino_ref, go_ref, *rows_last_refs):
    cos, sa, sb = cos_ref[...], sa_ref[...], sb_ref[...]
    seg = seg_ref[...]
    half = ROT_DIM // 2

    def norm_rope(x, g):
        y = _head_rms(x, seg, g)
        return y * cos + pltpu.roll(y, half, 1) * sa + pltpu.roll(y, LANES - half, 1) * sb

    for c in range(MIXER_WIDTH // LANES):
        sl = slice(c * LANES, (c + 1) * LANES)
        qo_ref[:, sl] = norm_rope(q_ref[:, sl], gq_ref[...])
    for c in range(6):
        x = kv_ref[:, c * LANES:(c + 1) * LANES]
        if c % 2 == 0:
            x = norm_rope(x, gkv_ref[c // 2])
        dst, cc = (cso_ref, c) if c < 4 else (wino_ref, c - 4)
        dst[:, cc * LANES:(cc + 1) * LANES] = x
        if rows_last_refs:
            dst_t = rows_last_refs[0] if c < 4 else rows_last_refs[1]
            dst_t[cc * LANES:(cc + 1) * LANES, :] = x.T
    go_ref[...] = _sigmoid(gt_ref[...])


def nsa_prep(u, rope_tabs, gq, gkv, seg, n_time_tiles, tm, rows_last=False):
    n = u.shape[0]
    cos, sa, sb = rope_tabs
    tab = pl.BlockSpec((tm, LANES), lambda i: (i % n_time_tiles, 0))
    const = lambda *shape: pl.BlockSpec(shape, lambda i: (0,) * len(shape))
    out_specs = [pl.BlockSpec((tm, MIXER_WIDTH), lambda i: (i, 0)),
                 pl.BlockSpec((tm, 4 * LANES), lambda i: (i, 0)),
                 pl.BlockSpec((tm, 2 * LANES), lambda i: (i, 0)),
                 pl.BlockSpec((tm, LANES), lambda i: (i, 0))]
    out_shape = [jax.ShapeDtypeStruct((n, MIXER_WIDTH), F32),
                 jax.ShapeDtypeStruct((n, 4 * LANES), F32),
                 jax.ShapeDtypeStruct((n, 2 * LANES), F32),
                 jax.ShapeDtypeStruct((n, LANES), F32)]
    if rows_last:
        n_batch, seq = n // (n_time_tiles * tm), n_time_tiles * tm
        for width in (4 * LANES, 2 * LANES):
            out_specs.append(pl.BlockSpec((None, width, tm), lambda i: (i // n_time_tiles, 0, i % n_time_tiles)))
            out_shape.append(jax.ShapeDtypeStruct((n_batch, width, seq), F32))
    return pl.pallas_call(
        _nsa_prep_kernel,
        grid=(n // tm,),
        in_specs=[pl.BlockSpec((tm, MIXER_WIDTH), lambda i: (i, 0)),
                  pl.BlockSpec((tm, MIXER_WIDTH), lambda i: (i, 1)),
                  pl.BlockSpec((tm, LANES), lambda i: (i, (2 * MIXER_WIDTH + MEM_WIDTH) // LANES)),
                  tab, tab, tab, const(1, LANES), const(3, 1, LANES), const(LANES, LANES)],
        out_specs=out_specs,
        out_shape=out_shape,
        compiler_params=_params("parallel"),
        name="nsa_prep",
    )(u, u, u, cos, sa, sb, gq, gkv, seg)


def _compress_kernel(xk_ref, xv_ref, w_ref, o_ref):
    mb = o_ref.shape[0]
    for c, x_ref in enumerate((xk_ref, xv_ref)):
        acc = jnp.zeros((mb, LANES), F32)
        for r in range(NSA_BLOCK):
            xr = x_ref[pl.ds(r, mb, stride=NSA_BLOCK), :]
            acc = acc + jnp.dot(xr.astype(BF16), w_ref[c, r], preferred_element_type=F32)
        o_ref[:, c * LANES:(c + 1) * LANES] = acc


def compress(cs, wbd):
    nblk = cs.shape[0] // NSA_BLOCK
    mb = _pick_tile(nblk, CMP_BLOCKS)
    return pl.pallas_call(
        _compress_kernel,
        grid=(nblk // mb,),
        in_specs=[pl.BlockSpec((mb * NSA_BLOCK, LANES), lambda i: (i, 0)),
                  pl.BlockSpec((mb * NSA_BLOCK, LANES), lambda i: (i, 1)),
                  pl.BlockSpec(wbd.shape, lambda i: (0, 0, 0, 0))],
        out_specs=pl.BlockSpec((mb, 2 * LANES), lambda i: (i, 0)),
        out_shape=jax.ShapeDtypeStruct((nblk, 2 * LANES), F32),
        compiler_params=_params("parallel"),
        name="nsa_compress",
    )(cs, cs, wbd)


def _compress_pool_kernel(x_ref, w_ref, o_ref):
    mbp = o_ref.shape[0]
    low = _iota((1, LANES), 1) < HEAD_DIM
    for c in range(2):
        slabs = []
        for g in range(NSA_GROUPS):
            acc = jnp.zeros((mbp, LANES), F32)
            for d in range(HEAD_DIM):
                xr = x_ref[:, c * LANES + g * HEAD_DIM + d, :]
                acc = acc + jnp.dot(xr.astype(BF16), w_ref[c, d], preferred_element_type=F32)
            slabs.append(acc)
        first = jnp.where(low, slabs[0], pltpu.roll(slabs[1], HEAD_DIM, 1))
        second = jnp.where(low, pltpu.roll(slabs[0], HEAD_DIM, 1), slabs[1])
        o_ref[:, c * LANES:(c + 1) * LANES] = first
        o_ref[:, (2 + c) * LANES:(3 + c) * LANES] = second


def compress_pool(pool_t, layer, wt):
    n_pool = pool_t.shape[1]
    mbp = _pick_tile(n_pool, CMP_PAGES)
    return pl.pallas_call(
        _compress_pool_kernel,
        grid=(n_pool // mbp,),
        in_specs=[pl.BlockSpec((None, mbp, 2 * LANES, PAGE_SIZE), lambda i: (layer, i, 0, 0)),
                  pl.BlockSpec(wt.shape, lambda i: (0, 0, 0, 0))],
        out_specs=pl.BlockSpec((mbp, 4 * LANES), lambda i: (i, 0)),
        out_shape=jax.ShapeDtypeStruct((n_pool, 4 * LANES), F32),
        compiler_params=_params("parallel"),
        name="nsa_compress_pool",
    )(pool_t, wt)


def _group_queries(q_chunks, g):
    parts = []
    for h in range(NSA_HPG):
        hh = g * NSA_HPG + h
        chunk = q_chunks(hh // 2)
        if hh % 2 != g:
            chunk = pltpu.roll(chunk, HEAD_DIM, 1)
        parts.append(chunk)
    qz = jnp.concatenate(parts, axis=0) * SCORE_SCALE
    return jnp.where((_iota((1, LANES), 1) >> 6) == g, qz, 0.0)


def _select_blocks(imp, cur):
    r, nb = imp.shape
    nidx = _iota((1, nb), 1)
    score = jnp.where(nidx == cur, BIG, jnp.where(nidx < cur, imp, -1.0))
    rank = jnp.zeros((r, nb), F32)
    for m in range(nb):
        col = score[:, m:m + 1]
        rank = rank + jnp.where(nidx > m, jnp.where(col >= score, 1.0, 0.0), jnp.where(col > score, 1.0, 0.0))
    return (rank < float(NSA_TOPK)) & (nidx <= cur)


def _select_blocks_t(imp, cur, live_blocks, score_ref):
    nb, r = imp.shape
    nidx = _iota((nb, 1), 0)
    score = jnp.where(nidx == cur, BIG, jnp.where(nidx < cur, imp, -1.0))
    score_ref[...] = score

    def count_ahead(m, rank):
        row = score_ref[pl.ds(m, 1), :]
        return rank + jnp.where(nidx > m, jnp.where(row >= score, 1.0, 0.0), jnp.where(row > score, 1.0, 0.0))

    rank = lax.fori_loop(0, live_blocks, count_ahead, jnp.zeros((nb, r), F32))
    return jnp.where((rank < float(NSA_TOPK)) & (nidx <= cur), 1.0, 0.0)


def _masked_softmax_parts(s, ok):
    m = jnp.max(jnp.where(ok, s, -BIG), axis=-1, keepdims=True)
    p = jnp.where(ok, jnp.exp(s - m), 0.0)
    return p, jnp.maximum(jnp.sum(p, axis=-1, keepdims=True), 1e-30)


def _place_heads(res, g, rows):
    low = _iota((1, LANES), 1) < HEAD_DIM
    out = []
    for c in range(NSA_HPG // 2):
        a = res[(2 * c) * rows:(2 * c + 1) * rows]
        b = res[(2 * c + 1) * rows:(2 * c + 2) * rows]
        if g == 1:
            a = pltpu.roll(a, HEAD_DIM, 1)
        else:
            b = pltpu.roll(b, HEAD_DIM, 1)
        out.append(jnp.where(low, a, b))
    return out


def _nsa_attn_kernel(q_ref, g_ref, cs_ref, win_ref, cmp_ref, cmpt_ref, gexp_ref, o_ref, score_ref):
    qt = pl.program_id(1)
    t0 = qt * Q_TILE
    seq = cs_ref.shape[0]
    nb = cmp_ref.shape[0]
    tpos = t0 + _iota((Q_TILE, 1), 0)
    rows = NSA_HPG * Q_TILE

    gates = g_ref[...]
    gexp = [_dot_x2(gates, gexp_ref[j]) for j in range(3)]

    win_span = min(NSA_WINDOW + Q_TILE, seq)
    w0 = pl.multiple_of(jnp.maximum(t0 + Q_TILE - win_span, 0), Q_TILE)
    n_chunks = (t0 + Q_TILE + SEL_CHUNK - 1) // SEL_CHUNK

    t_lane = t0 + (_iota((1, rows), 1) & (Q_TILE - 1))
    block_row = _iota((nb, 1), 0)
    complete = (block_row + 1) * NSA_BLOCK <= t_lane + 1
    cur_lane = (t0 + _iota((1, Q_TILE), 1)) >> 6
    group_lane = _iota((1, LANES), 1) >> 6
    live_blocks = jnp.minimum((t0 + Q_TILE - 1) // NSA_BLOCK + 1, nb)

    qzb, sel_bf, o_cmp = [], [], []
    for g in range(NSA_GROUPS):
        qz = _group_queries(lambda c: q_ref[:, c * LANES:(c + 1) * LANES], g)
        qzb.append(qz.astype(BF16))
        s = _dot_nt_precise(cmp_ref[:, 0:LANES], qz)
        m_cmp = jnp.max(jnp.where(complete, s, -BIG), axis=0, keepdims=True)
        p = jnp.where(complete, jnp.exp(s - m_cmp), 0.0)
        p_cmp = p / jnp.maximum(jnp.sum(p, axis=0, keepdims=True), 1e-30)
        o_cmp_t = _dot(cmpt_ref[LANES:2 * LANES, :], p_cmp)
        o_cmp.append(jnp.concatenate([o_cmp_t[:, h * Q_TILE:(h + 1) * Q_TILE].T for h in range(NSA_HPG)], axis=0))
        importance = sum(p_cmp[:, h * Q_TILE:(h + 1) * Q_TILE] for h in range(NSA_HPG))
        sel_t = _select_blocks_t(importance, cur_lane, live_blocks, score_ref)
        sel = jnp.concatenate([sel_t, jnp.zeros((LANES - nb, Q_TILE), F32)], axis=0).T
        sel_bf.append(sel.astype(BF16))

    def with_ones(v, g):
        return jnp.where(group_lane == g, v, 1.0).astype(BF16)

    def normalised(res):
        return res / pltpu.roll(res, HEAD_DIM, 1)

    def sel_chunk(kc, carry):
        k0 = pl.multiple_of(kc * SEL_CHUNK, SEL_CHUNK)
        kch = cs_ref[pl.ds(k0, SEL_CHUNK), 2 * LANES:3 * LANES].astype(BF16)
        vch = cs_ref[pl.ds(k0, SEL_CHUNK), 3 * LANES:4 * LANES]
        kpos = k0 + _iota((1, SEL_CHUNK), 1)
        expand = jnp.where((kpos >> 6) == _iota((LANES, 1), 0), 1.0, 0.0).astype(BF16)
        causal = kpos <= tpos
        out = []
        for g, (m, acc) in enumerate(carry):
            chosen = jnp.dot(sel_bf[g], expand, preferred_element_type=F32)
            bias = jnp.where((chosen > 0.5) & causal, 0.0, -BIG)[None]
            sc = lax.dot_general(qzb[g], kch, NT_DIMS, preferred_element_type=F32
                                 ).reshape(NSA_HPG, Q_TILE, SEL_CHUNK) + bias
            m_new = jnp.maximum(m, jnp.max(sc, axis=-1, keepdims=True))
            pp = jnp.exp(sc - m_new).astype(BF16)
            pv = jnp.dot(pp.reshape(rows, SEL_CHUNK), with_ones(vch, g), preferred_element_type=F32)
            out.append((m_new, jnp.exp(m - m_new) * acc + pv.reshape(NSA_HPG, Q_TILE, LANES)))
        return tuple(out)

    init = tuple((jnp.full((NSA_HPG, Q_TILE, 1), -BIG, F32), jnp.zeros((NSA_HPG, Q_TILE, LANES), F32))
                 for _ in range(NSA_GROUPS))
    sel_state = lax.fori_loop(0, n_chunks, sel_chunk, init)

    kw = win_ref[pl.ds(w0, win_span), 0:LANES].astype(BF16)
    vw = win_ref[pl.ds(w0, win_span), LANES:2 * LANES]
    wpos = w0 + _iota((1, win_span), 1)
    w_bias = jnp.where((wpos <= tpos) & (wpos > tpos - NSA_WINDOW), 0.0, -BIG)[None]

    branch_out = []
    for g in range(NSA_GROUPS):
        o_slc = normalised(sel_state[g][1].reshape(rows, LANES))
        sw = lax.dot_general(qzb[g], kw, NT_DIMS, preferred_element_type=F32
                             ).reshape(NSA_HPG, Q_TILE, win_span) + w_bias
        pw = jnp.exp(sw - jnp.max(sw, axis=-1, keepdims=True)).astype(BF16)
        o_win = normalised(jnp.dot(pw.reshape(rows, win_span), with_ones(vw, g), preferred_element_type=F32))
        branch_out.append([_place_heads(o, g, Q_TILE) for o in (o_cmp[g], o_slc, o_win)])

    for c in range(MIXER_WIDTH // LANES):
        g, cc = divmod(c, NSA_HPG // 2)
        sl = slice(c * LANES, (c + 1) * LANES)
        o_ref[:, sl] = sum(gexp[j][:, sl] * branch_out[g][j][cc] for j in range(3))


def nsa_attn_prompt(q, gates, cs, win, cmp, gexp, n_batch):
    n = q.shape[0]
    t = n // n_batch
    nt = t // Q_TILE
    nb = cmp.shape[0] // n_batch
    assert nb <= LANES
    cmp_t = cmp.reshape(n_batch, nb, 2 * LANES).transpose(0, 2, 1)
    return pl.pallas_call(
        _nsa_attn_kernel,
        grid=(n_batch, nt),
        in_specs=[pl.BlockSpec((Q_TILE, MIXER_WIDTH), lambda b, i: (b * nt + i, 0)),
                  pl.BlockSpec((Q_TILE, LANES), lambda b, i: (b * nt + i, 0)),
                  pl.BlockSpec((t, 4 * LANES), lambda b, i: (b, 0)),
                  pl.BlockSpec((t, 2 * LANES), lambda b, i: (b, 0)),
                  pl.BlockSpec((nb, 2 * LANES), lambda b, i: (b, 0)),
                  pl.BlockSpec((None, 2 * LANES, nb), lambda b, i: (b, 0, 0)),
                  pl.BlockSpec(gexp.shape, lambda b, i: (0, 0, 0))],
        out_specs=pl.BlockSpec((Q_TILE, MIXER_WIDTH), lambda b, i: (b * nt + i, 0)),
        out_shape=jax.ShapeDtypeStruct((n, MIXER_WIDTH), F32),
        scratch_shapes=[pltpu.VMEM((nb, Q_TILE), F32)],
        compiler_params=_params("parallel", "parallel"),
        name="nsa_attn_prompt",
    )(q, gates, cs, win, cmp, cmp_t, gexp)


SAMPLE_ROWS = 16
NB_PAD = 64


def _sample_queries(q_row):
    row = _iota((SAMPLE_ROWS, 1), 0)
    lane_group = _iota((1, LANES), 1) >> 6
    qz = jnp.zeros((SAMPLE_ROWS, LANES), F32)
    for hh in range(NSA_HEADS):
        g = hh // NSA_HPG
        chunk = jnp.broadcast_to(q_row[:, (hh // 2) * LANES:(hh // 2 + 1) * LANES], (SAMPLE_ROWS, LANES))
        if hh % 2 != g:
            chunk = pltpu.roll(chunk, HEAD_DIM, 1)
        qz = qz + jnp.where((row == hh) & (lane_group == g), chunk, 0.0)
    return qz * SCORE_SCALE


def _row_to_column(row):
    n = row.shape[1]
    diag = _iota((n, 1), 0) == _iota((1, n), 1)
    return jnp.sum(jnp.where(diag, jnp.broadcast_to(row, (n, n)), 0.0), axis=1, keepdims=True)


def _attend_with_new_key(s, ok, qz, k_new, v_new, weighted_values):
    s_new = jnp.sum(_bf16_round(qz) * _bf16_round(k_new), axis=-1, keepdims=True)
    m_lanes = jnp.max(jnp.where(ok, s, -BIG), axis=-1, keepdims=True)
    if s.ndim == 3:
        m_lanes = jnp.max(m_lanes, axis=0)
    m = jnp.maximum(m_lanes, s_new)
    p = jnp.where(ok, jnp.exp(s - m), 0.0)
    p_new = jnp.exp(s_new - m)
    total = jnp.sum(p, axis=-1, keepdims=True)
    if s.ndim == 3:
        total = jnp.sum(total, axis=0)
    den = total + p_new
    return (weighted_values(p) + _bf16_round(p_new) * _bf16_round(v_new)) / den


def _nsa_sample_kernel(pt_ref, q_ref, cmp_ref, new_ref, win_ref, wnew_ref, g_ref, gexp_ref, *refs, past):
    n_pages = past // PAGE_SIZE
    page_refs = refs[:n_pages]
    o_ref, wout_ref, cm_ref = refs[n_pages:]
    b = pl.program_id(0)
    head = _iota((SAMPLE_ROWS, 1), 0)
    qz = _sample_queries(q_ref[0])
    qzb = qz.astype(BF16)

    cm_ref[...] = jnp.zeros(cm_ref.shape, F32)
    for p in range(n_pages):
        row = cmp_ref[pl.ds(pt_ref[b * n_pages + p], 1), :]
        cm_ref[2 * p:2 * p + 1, :] = row[:, 0:2 * LANES]
        cm_ref[2 * p + 1:2 * p + 2, :] = row[:, 2 * LANES:4 * LANES]
    s = _dot_nt_precise(qz, cm_ref[:, 0:LANES])
    nidx = _iota((1, NB_PAD), 1)
    complete = (nidx + 1) * NSA_BLOCK <= past + 1
    p, den = _masked_softmax_parts(s, complete)
    p_cmp = p / den
    o_cmp = _dot(p_cmp, cm_ref[:, LANES:2 * LANES])

    row8 = _iota((SUBLANES, 1), 0)
    imp = jnp.zeros((SUBLANES, NB_PAD), F32)
    for g in range(NSA_GROUPS):
        in_group = (head >= g * NSA_HPG) & (head < (g + 1) * NSA_HPG)
        imp = imp + jnp.where(row8 == g, jnp.sum(jnp.where(in_group, p_cmp, 0.0), axis=0, keepdims=True), 0.0)
    cur = jnp.full((SUBLANES, 1), past // NSA_BLOCK, jnp.int32)
    sel = jnp.where(_select_blocks(imp, cur), 1.0, 0.0)
    sel_rows = jnp.where(head < NSA_HPG, sel[0:1], sel[1:2]).astype(BF16)
    kpos = _iota((1, past), 1)
    expand = jnp.where((kpos >> 6) == _iota((NB_PAD, 1), 0), 1.0, 0.0).astype(BF16)
    chosen = jnp.dot(sel_rows, expand, preferred_element_type=F32)

    s3 = jnp.concatenate(
        [jnp.dot(qzb, page_refs[p][0:LANES, :].astype(BF16), preferred_element_type=F32) for p in range(n_pages)],
        axis=0).reshape(n_pages, SAMPLE_ROWS, PAGE_SIZE)
    ok3 = jnp.concatenate([chosen[:, p * PAGE_SIZE:(p + 1) * PAGE_SIZE] for p in range(n_pages)],
                          axis=0).reshape(n_pages, SAMPLE_ROWS, PAGE_SIZE) > 0.5

    def sel_values(pp):
        return sum(_dot_nt(pp[p], page_refs[p][LANES:2 * LANES, :]) for p in range(n_pages))

    o_slc = _attend_with_new_key(s3, ok3, qz, new_ref[0][:, 2 * LANES:3 * LANES], new_ref[0][:, 3 * LANES:4 * LANES],
                                 sel_values)

    lw = win_ref.shape[1]
    sw = jnp.dot(qzb, win_ref[0:LANES, :].astype(BF16), preferred_element_type=F32)
    okw = _iota((1, lw), 1) > lw - NSA_WINDOW
    o_win = _attend_with_new_key(sw, okw, qz, wnew_ref[0][:, 0:LANES], wnew_ref[0][:, LANES:2 * LANES],
                                 lambda pp: _dot_nt(pp, win_ref[LANES:2 * LANES, :]))

    gates = jnp.broadcast_to(g_ref[0], (SUBLANES, LANES))
    lane_head = _iota((1, MIXER_WIDTH), 1) >> 6
    in_place = (head & 1) == jnp.where(head >= NSA_HPG, 1, 0)
    out = jnp.zeros((1, MIXER_WIDTH), F32)
    for j, o in enumerate((o_cmp, o_slc, o_win)):
        tiled = jnp.concatenate([o] * (MIXER_WIDTH // LANES), axis=1)
        rolled = jnp.concatenate([pltpu.roll(o, HEAD_DIM, 1)] * (MIXER_WIDTH // LANES), axis=1)
        placed = jnp.where(lane_head == head, jnp.where(in_place, tiled, rolled), 0.0)
        out = out + _dot_x2(gates, gexp_ref[j])[0:1] * jnp.sum(placed, axis=0, keepdims=True)
    o_ref[0] = out

    new_col = jnp.broadcast_to(_row_to_column(wnew_ref[0]), (2 * LANES, LANES))
    last_lane = _iota((1, LANES), 1) == LANES - 1
    for c in range(lw // LANES):
        shifted = pltpu.roll(win_ref[:, c * LANES:(c + 1) * LANES], LANES - 1, 1)
        if (c + 1) * LANES < lw:
            carry_in = pltpu.roll(win_ref[:, (c + 1) * LANES:(c + 2) * LANES], LANES - 1, 1)
        else:
            carry_in = new_col
        wout_ref[:, c * LANES:(c + 1) * LANES] = jnp.where(last_lane, carry_in, shifted)


def nsa_sample(page_table, layer, q, cmp_pool, cs_new, win_t, win_new, gates, gexp, pool_t):
    s, n_pages = page_table.shape
    past = n_pages * PAGE_SIZE
    lw = win_t.shape[-1]
    per_seq = lambda *shape: pl.BlockSpec((1,) + shape, lambda b, pt: (b,) + (0,) * len(shape))
    page_spec = lambda p: pl.BlockSpec((None, None, 2 * LANES, PAGE_SIZE),
                                       lambda b, pt: (layer, pt[b * n_pages + p], 1, 0))
    grid_spec = pltpu.PrefetchScalarGridSpec(
        num_scalar_prefetch=1,
        grid=(s,),
        in_specs=[per_seq(1, MIXER_WIDTH),
                  pl.BlockSpec(cmp_pool.shape, lambda b, pt: (0, 0)),
                  per_seq(1, 4 * LANES),
                  pl.BlockSpec((None, None, 2 * LANES, lw), lambda b, pt: (layer, b, 0, 0)),
                  per_seq(1, 2 * LANES), per_seq(1, LANES),
                  pl.BlockSpec(gexp.shape, lambda b, pt: (0, 0, 0))]
                 + [page_spec(p) for p in range(n_pages)],
        out_specs=[per_seq(1, MIXER_WIDTH), pl.BlockSpec((None, 2 * LANES, lw), lambda b, pt: (b, 0, 0))],
        scratch_shapes=[pltpu.VMEM((NB_PAD, 2 * LANES), F32)],
    )
    return pl.pallas_call(
        functools.partial(_nsa_sample_kernel, past=past),
        grid_spec=grid_spec,
        out_shape=[jax.ShapeDtypeStruct((s, 1, MIXER_WIDTH), F32),
                   jax.ShapeDtypeStruct((s, 2 * LANES, lw), F32)],
        compiler_params=_params("arbitrary"),
        name="nsa_sample",
    )(page_table.reshape(-1), q, cmp_pool, cs_new, win_t, win_new, gates, gexp, *([pool_t] * n_pages))


def _layer_norm_swish(y, g, b):
    mu = jnp.mean(y, axis=-1, keepdims=True)
    var = jnp.mean(jnp.square(y - mu), axis=-1, keepdims=True)
    z = (y - mu) * lax.rsqrt(var + EPS) * g + b
    return z * _sigmoid(z)


def _conv_prompt_kernel(a_ref, b_ref, ap_ref, bp_ref, w_ref, cb_ref, lg_ref, lb_ref, o_ref, st_ref, buf_ref):
    i = pl.program_id(1)
    glu = a_ref[...] * _sigmoid(b_ref[...])
    prev = ap_ref[...] * _sigmoid(bp_ref[...])
    buf_ref[0:CONV_HALO, :] = jnp.where(i > 0, prev, 0.0)
    buf_ref[CONV_HALO:, :] = glu
    tt = a_ref.shape[0]
    acc = jnp.zeros(glu.shape, F32)
    for k in range(CONV_WIDTH):
        acc = acc + w_ref[k:k + 1, :] * buf_ref[pl.ds(CONV_HALO - (CONV_WIDTH - 1) + k, tt), :]
    o_ref[...] = _layer_norm_swish(acc + cb_ref[...], lg_ref[...], lb_ref[...])

    @pl.when(i == pl.num_programs(1) - 1)
    def _():
        st_ref[0] = glu[tt - CONV_HALO:, :]


def conv_prompt(u, w, cb, lg, lb, n_batch):
    n = u.shape[0]
    t = n // n_batch
    tt = min(CONV_TILE, t)
    nt = t // tt
    ch = MIXER_WIDTH
    ratio = tt // CONV_HALO
    cur = lambda col: pl.BlockSpec((tt, ch), lambda b, i: (b * nt + i, col))
    prev = lambda col: pl.BlockSpec((CONV_HALO, ch), lambda b, i: (jnp.maximum((b * nt + i) * ratio - 1, 0), col))
    const = lambda *shape: pl.BlockSpec(shape, lambda b, i: (0,) * len(shape))
    return pl.pallas_call(
        _conv_prompt_kernel,
        grid=(n_batch, nt),
        in_specs=[cur(0), cur(1), prev(0), prev(1), const(CONV_HALO, ch), const(1, ch), const(1, ch), const(1, ch)],
        out_specs=[pl.BlockSpec((tt, ch), lambda b, i: (b * nt + i, 0)),
                   pl.BlockSpec((1, CONV_HALO, ch), lambda b, i: (b, 0, 0))],
        out_shape=[jax.ShapeDtypeStruct((n, ch), F32),
                   jax.ShapeDtypeStruct((n_batch, CONV_HALO, ch), F32)],
        scratch_shapes=[pltpu.VMEM((CONV_HALO + tt, ch), F32)],
        compiler_params=_params("parallel", "arbitrary"),
        name="conv_prompt",
    )(u, u, u, u, w, cb, lg, lb)


def _conv_sample_kernel(a_ref, b_ref, st_ref, w_ref, cb_ref, lg_ref, lb_ref, o_ref, glu_ref):
    glu = a_ref[...] * _sigmoid(b_ref[...])
    acc = w_ref[CONV_WIDTH - 1:CONV_WIDTH, :] * glu
    for k in range(CONV_WIDTH - 1):
        acc = acc + w_ref[k:k + 1, :] * st_ref[k]
    o_ref[...] = _layer_norm_swish(acc + cb_ref[...], lg_ref[...], lb_ref[...])
    glu_ref[...] = glu


def conv_sample(u, state_t, layer, w, cb, lg, lb):
    s = u.shape[0]
    ch = MIXER_WIDTH
    const = lambda *shape: pl.BlockSpec(shape, lambda i: (0,) * len(shape))
    return pl.pallas_call(
        _conv_sample_kernel,
        grid=(1,),
        in_specs=[pl.BlockSpec((s, ch), lambda i: (0, 0)), pl.BlockSpec((s, ch), lambda i: (0, 1)),
                  pl.BlockSpec((None, CONV_WIDTH - 1, s, ch), lambda i: (layer, 0, 0, 0)),
                  const(CONV_HALO, ch), const(1, ch), const(1, ch), const(1, ch)],
        out_specs=[const(s, ch), const(s, ch)],
        out_shape=[jax.ShapeDtypeStruct((s, ch), F32), jax.ShapeDtypeStruct((s, ch), F32)],
        compiler_params=_params("arbitrary"),
        name="conv_sample",
    )(u, u, state_t, w, cb, lg, lb)


SB_PAIRS = MIXER_WIDTH // LANES


def _stick_weights(z, valid, later, tri):
    sp = _softplus(z)
    log_stay = jnp.where(valid, -sp, 0.0)
    within = _dot_x2(log_stay, tri)
    a = jnp.where(valid, jnp.exp(z - sp + within + later), 0.0)
    return a, within[:, 0:1] + log_stay[:, 0:1]


def _sb_attn_kernel(q_ref, k_ref, v_ref, tri_ref, o_ref, acc_ref):
    qt = pl.program_id(1)
    t0 = qt * Q_TILE
    low = _iota((1, LANES), 1) < HEAD_DIM
    pair_rows = 2 * Q_TILE
    tpos = t0 + (_iota((SB_PAIRS * pair_rows, 1), 0) & (Q_TILE - 1))
    tri = tri_ref[...]
    qz = []
    for pr in range(SB_PAIRS):
        q2 = q_ref[:, pr * LANES:(pr + 1) * LANES] * SCORE_SCALE
        qz.append(jnp.concatenate([jnp.where(low, q2, 0.0), jnp.where(low, 0.0, q2)], axis=0).astype(BF16))
    acc_ref[...] = jnp.zeros(acc_ref.shape, F32)

    def alive(carry):
        return (carry[0] <= qt) & (carry[1] > 0)

    def chunk(carry):
        i, _, later = carry
        k0 = pl.multiple_of((qt - i) * SB_CHUNK, SB_CHUNK)
        valid = (k0 + _iota((1, SB_CHUNK), 1)) < tpos
        z = jnp.concatenate(
            [lax.dot_general(qz[pr], k_ref[pl.ds(k0, SB_CHUNK), pr * LANES:(pr + 1) * LANES], NT_DIMS,
                             preferred_element_type=F32) for pr in range(SB_PAIRS)], axis=0)
        a, total = _stick_weights(z, valid, later, tri)
        a = a.astype(BF16)
        for pr in range(SB_PAIRS):
            acc_ref[pr] += jnp.dot(a[pr * pair_rows:(pr + 1) * pair_rows],
                                   v_ref[pl.ds(k0, SB_CHUNK), pr * LANES:(pr + 1) * LANES],
                                   preferred_element_type=F32)
        later = later + total
        live = jnp.max(later) > UNDERFLOW_LOG
        return i + 1, live.astype(jnp.int32), later

    init = (jnp.int32(0), jnp.int32(1), jnp.zeros((SB_PAIRS * pair_rows, 1), F32))
    lax.while_loop(alive, chunk, init)
    for pr in range(SB_PAIRS):
        o_ref[:, pr * LANES:(pr + 1) * LANES] = jnp.where(low, acc_ref[pr, :Q_TILE], acc_ref[pr, Q_TILE:])


def sb_attn_prompt(u, kv16, tri, n_batch):
    n = u.shape[0]
    t = n // n_batch
    nt = t // Q_TILE
    return pl.pallas_call(
        _sb_attn_kernel,
        grid=(n_batch, nt),
        in_specs=[pl.BlockSpec((Q_TILE, MIXER_WIDTH), lambda b, i: (b * nt + i, 0)),
                  pl.BlockSpec((t, MIXER_WIDTH), lambda b, i: (b, 0)),
                  pl.BlockSpec((t, MIXER_WIDTH), lambda b, i: (b, 1)),
                  pl.BlockSpec(tri.shape, lambda b, i: (0, 0))],
        out_specs=pl.BlockSpec((Q_TILE, MIXER_WIDTH), lambda b, i: (b * nt + i, 0)),
        out_shape=jax.ShapeDtypeStruct((n, MIXER_WIDTH), F32),
        scratch_shapes=[pltpu.VMEM((SB_PAIRS, 2 * Q_TILE, LANES), F32)],
        compiler_params=_params("parallel", "parallel"),
        name="sb_attn_prompt",
    )(u, kv16, kv16, tri)


def _sb_sample_kernel(pt_ref, q_ref, tri_ref, suffix_ref, *refs):
    page_refs, o_ref = refs[:-1], refs[-1]
    n_pages = len(page_refs)
    head = _iota((SAMPLE_ROWS, 1), 0)
    lane_head = _iota((1, MIXER_WIDTH), 1) >> 6
    qz = (jnp.where(lane_head == head, jnp.broadcast_to(q_ref[0], (SAMPLE_ROWS, MIXER_WIDTH)), 0.0)
          * SCORE_SCALE).astype(BF16)
    z = jnp.concatenate(
        [jnp.dot(qz, page_refs[p][0:MIXER_WIDTH, :].astype(BF16), preferred_element_type=F32) for p in range(n_pages)],
        axis=0)
    sp = _softplus(z)
    log_stay = -sp
    within = _dot_x3(log_stay, tri_ref[...])
    total = jnp.broadcast_to(within[:, 0:1] + log_stay[:, 0:1], z.shape)
    later = _dot_w3(suffix_ref[...], total)
    a = jnp.exp(z - sp + within + later)
    acc = sum(_dot_nt(a[p * SAMPLE_ROWS:(p + 1) * SAMPLE_ROWS], page_refs[p][MIXER_WIDTH:2 * MIXER_WIDTH, :])
              for p in range(n_pages))
    o_ref[0] = jnp.sum(jnp.where(lane_head == head, acc, 0.0), axis=0, keepdims=True)


def sb_attn_sample(page_table, layer, q, pool_t, tri, suffix):
    s, n_pages = page_table.shape
    page_spec = lambda p: pl.BlockSpec((None, None, 2 * MIXER_WIDTH, PAGE_SIZE),
                                       lambda b, pt: (layer, pt[b * n_pages + p], 0, 0))
    grid_spec = pltpu.PrefetchScalarGridSpec(
        num_scalar_prefetch=1,
        grid=(s,),
        in_specs=[pl.BlockSpec((1, 1, MIXER_WIDTH), lambda b, pt: (b, 0, 0)),
                  pl.BlockSpec(tri.shape, lambda b, pt: (0, 0)),
                  pl.BlockSpec(suffix.shape, lambda b, pt: (0, 0))]
                 + [page_spec(p) for p in range(n_pages)],
        out_specs=pl.BlockSpec((1, 1, MIXER_WIDTH), lambda b, pt: (b, 0, 0)),
    )
    return pl.pallas_call(
        _sb_sample_kernel,
        grid_spec=grid_spec,
        out_shape=jax.ShapeDtypeStruct((s, 1, MIXER_WIDTH), F32),
        compiler_params=_params("arbitrary"),
        name="sb_attn_sample",
    )(page_table.reshape(-1), q, tri, suffix, *([pool_t] * n_pages))


def _rope_tables(positions):
    half = ROT_DIM // 2
    inv = np.exp(-math.log(ROPE_THETA) * np.arange(0, ROT_DIM, 2, dtype=np.float64) / ROT_DIM)
    ang = np.asarray(positions, np.float64)[:, None] * inv[None, :]
    lane = np.arange(LANES) % HEAD_DIM
    cos = np.ones((len(positions), LANES))
    sa = np.zeros((len(positions), LANES))
    sb = np.zeros((len(positions), LANES))
    for l in range(LANES):
        r = lane[l]
        if r < half:
            cos[:, l] = np.cos(ang[:, r])
            sb[:, l] = -np.sin(ang[:, r])
        elif r < ROT_DIM:
            cos[:, l] = np.cos(ang[:, r - half])
            sa[:, l] = np.sin(ang[:, r - half])
    return tuple(jnp.asarray(a, F32) for a in (cos, sa, sb))


def _segment_ones():
    lane = np.arange(LANES)
    return jnp.asarray(lane[:, None] // HEAD_DIM == lane[None, :] // HEAD_DIM, BF16)


def _gate_expand():
    e = np.zeros((3, LANES, MIXER_WIDTH), np.float32)
    for j in range(3):
        for h in range(NSA_HEADS):
            e[j, 3 * h + j, h * HEAD_DIM:(h + 1) * HEAD_DIM] = 1.0
    return jnp.asarray(e, BF16)


def _later_ones(n):
    idx = np.arange(n)
    return jnp.asarray(idx[:, None] > idx[None, :], BF16)


def _later_pages(n_pages):
    p = np.repeat(np.arange(n_pages), SAMPLE_ROWS)
    r = np.tile(np.arange(SAMPLE_ROWS), n_pages)
    return jnp.asarray((r[:, None] == r[None, :]) & (p[None, :] > p[:, None]), BF16)


def _compress_weights(w_cmp):
    w = w_cmp.reshape(2, NSA_BLOCK, HEAD_DIM, HEAD_DIM)
    return (jnp.tile(w, (1, 1, NSA_GROUPS, NSA_GROUPS)) * _segment_ones()).astype(BF16)


def _compress_weights_t(w_cmp):
    assert NSA_BLOCK == HEAD_DIM and PAGE_SIZE == 2 * NSA_BLOCK
    w = w_cmp.reshape(2, NSA_BLOCK, HEAD_DIM, HEAD_DIM).transpose(0, 2, 1, 3)
    return (jnp.tile(w, (1, 1, 2, 2)) * _segment_ones()).astype(BF16)


def _rows_last(x):
    a, b, rows = x.shape[:3]
    perm = (0, 1) + tuple(range(3, x.ndim)) + (2,)
    return x.transpose(perm).reshape(a, b, -1, rows)


def kernel(x_prompt, x_sample, cache_nsa_kv, state_nsa_win, state_conv, cache_sb_kv, cache_mem_kv, page_table,
           mem_prompt, norm_mix, norm_ffn, norm_mem, w_in_nsa, w_in_conv, w_in_sb, w_out, w_mem_kv, qk_norm_nsa,
           qk_norm_mem, w_nsa_cmp, conv_w, conv_b, conv_ln_g, conv_ln_b, w_ffn_in, w_ffn_out):
    n_batch, seq, d_model = x_prompt.shape
    n_dec = x_sample.shape[0]
    depth = w_out.shape[0]
    n_pages = page_table.shape[1]
    past = n_pages * PAGE_SIZE
    n_mem = mem_prompt.shape[1]
    ffn_hidden = w_ffn_out.shape[1]
    assert x_sample.shape[1] == 1 and seq % CONV_TILE == 0 and seq >= NSA_WINDOW + Q_TILE
    assert past % NSA_BLOCK == 0 and ffn_hidden % LANES == 0 and n_dec % MEM_BATCH == 0
    assert PAGE_SIZE == SB_CHUNK == LANES and state_nsa_win.shape[2] % LANES == 0

    xp = x_prompt.reshape(n_batch * seq, d_model)
    xs = x_sample.reshape(n_dec, d_model)

    seg = _segment_ones()
    gexp = _gate_expand()
    tri = _later_ones(SB_CHUNK)
    suffix = _later_pages(n_pages)
    rope_p = _rope_tables(np.arange(seq))
    rope_s = _rope_tables(np.full((n_dec,), past))

    nsa_pool_t = _rows_last(cache_nsa_kv)
    nsa_win_t = _rows_last(state_nsa_win)
    sb_pool_t = _rows_last(cache_sb_kv)
    mem_cache_t = _rows_last(cache_mem_kv)
    conv_state_t = state_conv.transpose(0, 2, 1, 3)

    nq, nkv = MIXER_WIDTH, 6 * NSA_GROUPS * HEAD_DIM
    n_gate = 3 * NSA_HEADS
    w_nsa = jnp.concatenate(
        [w_in_nsa[:, :, :nq + nkv], w_in_nsa[:, :, nq + nkv + n_gate:], w_in_nsa[:, :, nq + nkv:nq + nkv + n_gate],
         jnp.zeros(w_in_nsa.shape[:2] + (LANES - n_gate,), F32)], axis=-1).astype(BF16)
    w_conv = w_in_conv.astype(BF16)
    w_sb = w_in_sb.astype(BF16)
    w_o1 = w_out[:, :MIXER_WIDTH].astype(BF16)
    w_o2 = w_out[:, MIXER_WIDTH:].astype(BF16)
    w_gate_up = w_ffn_in.astype(BF16)
    w_down = w_ffn_out.astype(BF16)

    gq_mem = jnp.tile(qk_norm_mem[:, 0], (1, LANES // HEAD_DIM))
    gk_mem = jnp.tile(qk_norm_mem[:, 1], (1, LANES // HEAD_DIM))
    mem_kv = mem_kv_all(mem_prompt.reshape(n_batch * n_mem, d_model), norm_mem, w_mem_kv.astype(BF16), gk_mem, seg)

    nsa_p, nsa_s, win_p, win_s, conv_p, conv_s, sb_p, sb_s = [], [], [], [], [], [], [], []
    for layer in range(depth):
        kind, j = layer % N_MIXERS, layer // N_MIXERS
        if kind == 0:
            up = proj_in(xp, norm_mix[layer], w_nsa[j])
            us = proj_in(xs, norm_mix[layer], w_nsa[j])
            mem_col = 2 * MIXER_WIDTH // MEM_WIDTH
            gq = jnp.tile(qk_norm_nsa[j, 0], LANES // HEAD_DIM).reshape(1, LANES)
            gkv = jnp.tile(qk_norm_nsa[j, 1:4], (1, LANES // HEAD_DIM)).reshape(3, 1, LANES)

            q_p, cs_p, wn_p, gt_p, cs_pt, wn_pt = nsa_prep(up, rope_p, gq, gkv, seg, seq // 256, 256, rows_last=True)
            cmp_p = compress(cs_p, _compress_weights(w_nsa_cmp[j]))
            o_p = nsa_attn_prompt(q_p, gt_p, cs_p, wn_p, cmp_p, gexp, n_batch)

            q_s, cs_s, wn_s, gt_s = nsa_prep(us, rope_s, gq, gkv, seg, 1, n_dec)
            cmp_pool = compress_pool(nsa_pool_t, j, _compress_weights_t(w_nsa_cmp[j]))
            o_s, win_new_t = nsa_sample(page_table, j, q_s.reshape(n_dec, 1, MIXER_WIDTH), cmp_pool,
                                        cs_s.reshape(n_dec, 1, 4 * LANES), nsa_win_t,
                                        wn_s.reshape(n_dec, 1, 2 * LANES), gt_s.reshape(n_dec, 1, LANES), gexp,
                                        nsa_pool_t)
            o_s = o_s.reshape(n_dec, MIXER_WIDTH)

            lw = nsa_win_t.shape[-1]
            keep = min(NSA_WINDOW, seq)
            nsa_p.append(cs_pt.reshape(n_batch, 4, NSA_GROUPS, HEAD_DIM, seq).transpose(0, 4, 1, 2, 3))
            nsa_s.append(cs_s.reshape(n_dec, 1, 4, NSA_GROUPS, HEAD_DIM))
            win_p.append(wn_pt[:, :, seq - keep:].reshape(n_batch, 2, NSA_GROUPS, HEAD_DIM, keep).transpose(0, 4, 1, 2, 3))
            win_s.append(win_new_t.reshape(n_dec, 2, NSA_GROUPS, HEAD_DIM, lw).transpose(0, 4, 1, 2, 3))
        elif kind == 1:
            up = proj_in(xp, norm_mix[layer], w_conv[j])
            us = proj_in(xs, norm_mix[layer], w_conv[j])
            mem_col = 2 * MIXER_WIDTH // MEM_WIDTH
            cw = jnp.concatenate([conv_w[j], jnp.zeros((CONV_HALO - CONV_WIDTH, MIXER_WIDTH), F32)], axis=0)
            vecs = [v[j].reshape(1, MIXER_WIDTH) for v in (conv_b, conv_ln_g, conv_ln_b)]
            o_p, tail = conv_prompt(up, cw, *vecs, n_batch)
            o_s, glu_s = conv_sample(us, conv_state_t, j, cw, *vecs)
            conv_p.append(tail[:, CONV_HALO - (CONV_WIDTH - 1):])
            conv_s.append(jnp.concatenate([conv_state_t[j, 1:], glu_s[None]], axis=0).transpose(1, 0, 2))
        else:
            up, kv16, kv_t = proj_in(xp, norm_mix[layer], w_sb[j], narrow_cols=(MIXER_WIDTH, 3 * MIXER_WIDTH),
                                     n_batch=n_batch)
            us = proj_in(xs, norm_mix[layer], w_sb[j])
            mem_col = 3 * MIXER_WIDTH // MEM_WIDTH
            o_p = sb_attn_prompt(up, kv16, tri, n_batch)
            o_s = sb_attn_sample(page_table, j, us[:, :MIXER_WIDTH].reshape(n_dec, 1, MIXER_WIDTH), sb_pool_t, tri,
                                 suffix).reshape(n_dec, MIXER_WIDTH)
            sb_p.append(kv_t.reshape(n_batch, 2, NSA_HEADS, HEAD_DIM, seq).transpose(0, 4, 1, 2, 3))
            sb_s.append(us[:, MIXER_WIDTH:3 * MIXER_WIDTH].reshape(n_dec, 1, 2, NSA_HEADS, HEAD_DIM))

        gq_l = gq_mem[layer].reshape(1, LANES)
        om_p = mem_attn_prompt(up, mem_col, mem_kv[layer], gq_l, seg, n_batch)
        om_s = mem_attn_sample(us[:, mem_col * MEM_WIDTH:(mem_col + 1) * MEM_WIDTH], mem_cache_t, layer, gq_l, seg)

        ffn_w = (w_o1[layer], w_o2[layer], norm_ffn[layer], w_gate_up[layer], w_down[layer])
        xp = out_ffn(o_p, om_p, xp, *ffn_w)
        xs = out_ffn(o_s, om_s, xs, *ffn_w)

    return (xp.reshape(x_prompt.shape), xs.reshape(x_sample.shape),
            jnp.stack(nsa_p), jnp.stack(nsa_s), jnp.stack(win_p), jnp.stack(win_s),
            jnp.stack(conv_p), jnp.stack(conv_s), jnp.stack(sb_p), jnp.stack(sb_s),
            mem_kv.reshape(depth, n_batch, n_mem, 2, MEM_HEADS, HEAD_DIM))
```

```python
import functools
import math

import numpy as np
import jax
import jax.numpy as jnp
from jax import lax
from jax.experimental import pallas as pl
from jax.experimental.pallas import tpu as pltpu

F32 = jnp.float32
BF16 = jnp.bfloat16

HEAD_DIM = 64
MEM_HEADS = 4
MEM_WIDTH = MEM_HEADS * HEAD_DIM
N_MIXERS = 3
NSA_HEADS = 12
NSA_GROUPS = 2
NSA_HPG = NSA_HEADS // NSA_GROUPS
NSA_BLOCK = 64
NSA_TOPK = 16
NSA_WINDOW = 512
MIXER_WIDTH = NSA_HEADS * HEAD_DIM
CONV_WIDTH = 31
ROPE_THETA = 500000.0
ROT_DIM = HEAD_DIM // 4
EPS = 1e-6
PAGE_SIZE = 128

LANES = 128
SUBLANES = 8
VMEM_LIMIT = 56 * 1024 * 1024

ROW_TILE = 512
FFN_TILE = 256
Q_TILE = 128
SEL_CHUNK = 512
SB_CHUNK = 128
CONV_TILE = 512
CMP_BLOCKS = 256
CMP_PAGES = 128
MEM_BATCH = 8
NSA_SEQS_PER_STEP = 2
CONV_HALO = 32

SCORE_SCALE = HEAD_DIM ** -0.5
BIG = 1e30
UNDERFLOW_LOG = -104.0
NT_DIMS = (((1,), (1,)), ((), ()))


def _params(*sem):
    return pltpu.CompilerParams(dimension_semantics=sem, vmem_limit_bytes=VMEM_LIMIT)


def _pick_tile(n, cap):
    return max(t for t in range(SUBLANES, cap + 1, SUBLANES) if n % t == 0)


def _iota(shape, dim):
    return lax.broadcasted_iota(jnp.int32, shape, dim)


def _dot(a, b):
    return jnp.dot(a.astype(BF16), b.astype(BF16), preferred_element_type=F32)


def _dot_nt(a, b):
    return lax.dot_general(a.astype(BF16), b.astype(BF16), NT_DIMS, preferred_element_type=F32)


def _split2(x):
    hi = x.astype(BF16)
    lo = (x - hi.astype(F32)).astype(BF16)
    return hi, lo


def _split3(x):
    hi = x.astype(BF16)
    r = x - hi.astype(F32)
    mid = r.astype(BF16)
    lo = (r - mid.astype(F32)).astype(BF16)
    return hi, mid, lo


def _dot_x2(x, w_bf16):
    return sum(jnp.dot(part, w_bf16, preferred_element_type=F32) for part in _split2(x))


def _dot_x3(x, w_bf16):
    return sum(jnp.dot(part, w_bf16, preferred_element_type=F32) for part in _split3(x))


def _dot_w3(w_bf16, x):
    return sum(jnp.dot(w_bf16, part, preferred_element_type=F32) for part in _split3(x))


def _dot_nt_precise(a, b):
    ah, al = _split2(a)
    bh, bl = _split2(b)
    return (lax.dot_general(ah, bh, NT_DIMS, preferred_element_type=F32)
            + lax.dot_general(ah, bl, NT_DIMS, preferred_element_type=F32)
            + lax.dot_general(al, bh, NT_DIMS, preferred_element_type=F32))


def _bf16_round(x):
    return x.astype(BF16).astype(F32)


def _rms_rows(x, g):
    ms = jnp.mean(x * x, axis=-1, keepdims=True)
    return x * lax.rsqrt(ms + EPS) * g


def _head_rms(x, seg_ones, g):
    ms = _dot_x2(x * x, seg_ones) * (1.0 / HEAD_DIM)
    return x * lax.rsqrt(ms + EPS) * g


def _sigmoid(x):
    return 1.0 / (1.0 + jnp.exp(-x))


def _softplus(z):
    return jnp.maximum(z, 0.0) + jnp.log1p(jnp.exp(-jnp.abs(z)))


def _proj_in_kernel(x_ref, g_ref, w_ref, o_ref, *extra_refs, narrow_cols):
    h = _rms_rows(x_ref[...], g_ref[...]).astype(BF16)
    u = jnp.dot(h, w_ref[...], preferred_element_type=F32)
    o_ref[...] = u
    if narrow_cols is not None:
        narrow_ref, rows_last_ref = extra_refs
        narrow_ref[...] = u[:, narrow_cols[0]:narrow_cols[1]].astype(BF16)
        for c in range((narrow_cols[1] - narrow_cols[0]) // LANES):
            lo = narrow_cols[0] + c * LANES
            rows_last_ref[c * LANES:(c + 1) * LANES, :] = u[:, lo:lo + LANES].T


def proj_in(x, g, w, narrow_cols=None, n_batch=None):
    n, d = x.shape
    m = w.shape[1]
    tm = min(ROW_TILE, n)
    out_specs = [pl.BlockSpec((tm, m), lambda i: (i, 0))]
    out_shape = [jax.ShapeDtypeStruct((n, m), F32)]
    if narrow_cols is not None:
        width = narrow_cols[1] - narrow_cols[0]
        seq = n // n_batch
        nt = seq // tm
        out_specs.append(pl.BlockSpec((tm, width), lambda i: (i, 0)))
        out_shape.append(jax.ShapeDtypeStruct((n, width), BF16))
        out_specs.append(pl.BlockSpec((None, width, tm), lambda i: (i // nt, 0, i % nt)))
        out_shape.append(jax.ShapeDtypeStruct((n_batch, width, seq), F32))
    res = pl.pallas_call(
        functools.partial(_proj_in_kernel, narrow_cols=narrow_cols),
        grid=(n // tm,),
        in_specs=[pl.BlockSpec((tm, d), lambda i: (i, 0)),
                  pl.BlockSpec((1, d), lambda i: (0, 0)),
                  pl.BlockSpec((d, m), lambda i: (0, 0))],
        out_specs=out_specs,
        out_shape=out_shape,
        compiler_params=_params("parallel"),
        name="proj_in",
    )(x, g.reshape(1, d), w)
    return res if narrow_cols is not None else res[0]


def _out_ffn_kernel(o_ref, om_ref, x_ref, w1_ref, w2_ref, gf_ref, wgu_ref, wo_ref, y_ref):
    x = x_ref[...] + _dot(o_ref[...], w1_ref[...]) + _dot(om_ref[...], w2_ref[...])
    h = _rms_rows(x, gf_ref[...]).astype(BF16)
    hidden = wo_ref.shape[0]
    gate_up = jnp.dot(h, wgu_ref[...], preferred_element_type=F32)
    gate, up = gate_up[:, :hidden], gate_up[:, hidden:]
    act = (gate * _sigmoid(gate) * up).astype(BF16)
    y_ref[...] = x + jnp.dot(act, wo_ref[...], preferred_element_type=F32)


def out_ffn(o, om, x, w1, w2, gf, wgu, wo):
    n, d = x.shape
    tm = min(FFN_TILE, n)
    const = lambda *shape: pl.BlockSpec(shape, lambda i: (0,) * len(shape), pipeline_mode=pl.Buffered(1))
    return pl.pallas_call(
        _out_ffn_kernel,
        grid=(n // tm,),
        in_specs=[pl.BlockSpec((tm, o.shape[1]), lambda i: (i, 0)),
                  pl.BlockSpec((tm, om.shape[1]), lambda i: (i, 0)),
                  pl.BlockSpec((tm, d), lambda i: (i, 0)),
                  const(*w1.shape), const(*w2.shape), const(1, d), const(*wgu.shape), const(*wo.shape)],
        out_specs=pl.BlockSpec((tm, d), lambda i: (i, 0)),
        out_shape=jax.ShapeDtypeStruct((n, d), F32),
        compiler_params=_params("parallel"),
        name="out_ffn",
    )(o, om, x, w1, w2, gf.reshape(1, d), wgu, wo)


def _mem_kv_kernel(mem_ref, gn_ref, w_ref, gk_ref, seg_ref, o_ref):
    h = _rms_rows(mem_ref[...], gn_ref[0]).astype(BF16)
    kv = jnp.dot(h, w_ref[0], preferred_element_type=F32)
    seg = seg_ref[...]
    for c in range(MEM_WIDTH // LANES):
        sl = slice(c * LANES, (c + 1) * LANES)
        o_ref[0, :, sl] = _head_rms(kv[:, sl], seg, gk_ref[0])
    o_ref[0, :, MEM_WIDTH:] = kv[:, MEM_WIDTH:]


def mem_kv_all(mem, g_norm, w_kv, g_k, seg):
    n, d = mem.shape
    nl = w_kv.shape[0]
    return pl.pallas_call(
        _mem_kv_kernel,
        grid=(nl,),
        in_specs=[pl.BlockSpec((n, d), lambda l: (0, 0)),
                  pl.BlockSpec((1, 1, d), lambda l: (l, 0, 0)),
                  pl.BlockSpec((1, d, 2 * MEM_WIDTH), lambda l: (l, 0, 0)),
                  pl.BlockSpec((1, 1, LANES), lambda l: (l, 0, 0)),
                  pl.BlockSpec((LANES, LANES), lambda l: (0, 0))],
        out_specs=pl.BlockSpec((1, n, 2 * MEM_WIDTH), lambda l: (l, 0, 0)),
        out_shape=jax.ShapeDtypeStruct((nl, n, 2 * MEM_WIDTH), F32),
        compiler_params=_params("arbitrary"),
        name="mem_kv",
    )(mem, g_norm.reshape(nl, 1, d), w_kv, g_k.reshape(nl, 1, LANES), seg)


def _mem_query(qm, seg, gq):
    parts = [_head_rms(qm[:, c * LANES:(c + 1) * LANES], seg, gq) for c in range(MEM_WIDTH // LANES)]
    return jnp.concatenate(parts, axis=1) * SCORE_SCALE


def _softmax_rows(s):
    p = jnp.exp(s - jnp.max(s, axis=-1, keepdims=True))
    return p / jnp.sum(p, axis=-1, keepdims=True)


def _mem_attn_kernel(qm_ref, kv_ref, gq_ref, seg_ref, o_ref):
    qn = _mem_query(qm_ref[...], seg_ref[...], gq_ref[...])
    r = qn.shape[0]
    head_of_lane = _iota((1, MEM_WIDTH), 1) >> 6
    qz = jnp.concatenate([jnp.where(head_of_lane == h, qn, 0.0) for h in range(MEM_HEADS)], axis=0)
    p = _softmax_rows(_dot_nt(qz, kv_ref[:, :MEM_WIDTH]))
    res = _dot(p, kv_ref[:, MEM_WIDTH:])
    out = jnp.zeros((r, MEM_WIDTH), F32)
    for h in range(MEM_HEADS):
        out = out + jnp.where(head_of_lane == h, res[h * r:(h + 1) * r], 0.0)
    o_ref[...] = out


def mem_attn_prompt(u, mem_col, kv, gq, seg, n_batch):
    n = u.shape[0]
    t = n // n_batch
    tm = 256
    n_mem = kv.shape[0] // n_batch
    nt = t // tm
    return pl.pallas_call(
        _mem_attn_kernel,
        grid=(n_batch, nt),
        in_specs=[pl.BlockSpec((tm, MEM_WIDTH), lambda b, i: (b * nt + i, mem_col)),
                  pl.BlockSpec((n_mem, 2 * MEM_WIDTH), lambda b, i: (b, 0)),
                  pl.BlockSpec((1, LANES), lambda b, i: (0, 0)),
                  pl.BlockSpec((LANES, LANES), lambda b, i: (0, 0))],
        out_specs=pl.BlockSpec((tm, MEM_WIDTH), lambda b, i: (b * nt + i, 0)),
        out_shape=jax.ShapeDtypeStruct((n, MEM_WIDTH), F32),
        compiler_params=_params("parallel", "parallel"),
        name="mem_attn_prompt",
    )(u, kv, gq, seg)


def _mem_attn_sample_kernel(qm_ref, kv_ref, gq_ref, seg_ref, o_ref):
    qn = _mem_query(qm_ref[0], seg_ref[...], gq_ref[...])
    row = _iota((SUBLANES, 1), 0)
    head_of_lane = _iota((1, MEM_WIDTH), 1) >> 6
    out = jnp.zeros((MEM_BATCH, MEM_WIDTH), F32)
    for i in range(MEM_BATCH):
        qz = jnp.where(head_of_lane == (row & (MEM_HEADS - 1)), jnp.broadcast_to(qn[i:i + 1], (SUBLANES, MEM_WIDTH)), 0.0)
        p = _softmax_rows(_dot(qz, kv_ref[i, 0:MEM_WIDTH, :]))
        res = _dot_nt(p, kv_ref[i, MEM_WIDTH:2 * MEM_WIDTH, :])
        flat = jnp.sum(jnp.where(head_of_lane == row, res, 0.0), axis=0, keepdims=True)
        out = out + jnp.where(_iota((MEM_BATCH, 1), 0) == i, flat, 0.0)
    o_ref[0] = out


def mem_attn_sample(qm, kv_t, layer, gq, seg):
    _, s, _, n_mem = kv_t.shape
    nb = s // MEM_BATCH
    return pl.pallas_call(
        _mem_attn_sample_kernel,
        grid=(nb,),
        in_specs=[pl.BlockSpec((1, MEM_BATCH, MEM_WIDTH), lambda b: (b, 0, 0)),
                  pl.BlockSpec((None, MEM_BATCH, 2 * MEM_WIDTH, n_mem), lambda b: (layer, b, 0, 0)),
                  pl.BlockSpec((1, LANES), lambda b: (0, 0)),
                  pl.BlockSpec((LANES, LANES), lambda b: (0, 0))],
        out_specs=pl.BlockSpec((1, MEM_BATCH, MEM_WIDTH), lambda b: (b, 0, 0)),
        out_shape=jax.ShapeDtypeStruct((nb, MEM_BATCH, MEM_WIDTH), F32),
        compiler_params=_params("parallel"),
        name="mem_attn_sample",
    )(qm.reshape(nb, MEM_BATCH, MEM_WIDTH), kv_t, gq, seg).reshape(s, MEM_WIDTH)


def _nsa_prep_kernel(q_ref, kv_ref, gt_ref, cos_ref, sa_ref, sb_ref, gq_ref, gkv_ref, seg_ref,
                     qo_ref, cso_ref, wino_ref, go_ref, *rows_last_refs):
    cos, sa, sb = cos_ref[...], sa_ref[...], sb_ref[...]
    seg = seg_ref[...]
    half = ROT_DIM // 2

    def norm_rope(x, g):
        y = _head_rms(x, seg, g)
        return y * cos + pltpu.roll(y, half, 1) * sa + pltpu.roll(y, LANES - half, 1) * sb

    for c in range(MIXER_WIDTH // LANES):
        sl = slice(c * LANES, (c + 1) * LANES)
        qo_ref[:, sl] = norm_rope(q_ref[:, sl], gq_ref[...])
    for c in range(6):
        x = kv_ref[:, c * LANES:(c + 1) * LANES]
        if c % 2 == 0:
            x = norm_rope(x, gkv_ref[c // 2])
        dst, cc = (cso_ref, c) if c < 4 else (wino_ref, c - 4)
        dst[:, cc * LANES:(cc + 1) * LANES] = x
        if rows_last_refs:
            dst_t = rows_last_refs[0] if c < 4 else rows_last_refs[1]
            dst_t[cc * LANES:(cc + 1) * LANES, :] = x.T
    go_ref[...] = _sigmoid(gt_ref[...])


def nsa_prep(u, rope_tabs, gq, gkv, seg, n_time_tiles, tm, rows_last=False):
    n = u.shape[0]
    cos, sa, sb = rope_tabs
    tab = pl.BlockSpec((tm, LANES), lambda i: (i % n_time_tiles, 0))
    const = lambda *shape: pl.BlockSpec(shape, lambda i: (0,) * len(shape))
    out_specs = [pl.BlockSpec((tm, MIXER_WIDTH), lambda i: (i, 0)),
                 pl.BlockSpec((tm, 4 * LANES), lambda i: (i, 0)),
                 pl.BlockSpec((tm, 2 * LANES), lambda i: (i, 0)),
                 pl.BlockSpec((tm, LANES), lambda i: (i, 0))]
    out_shape = [jax.ShapeDtypeStruct((n, MIXER_WIDTH), F32),
                 jax.ShapeDtypeStruct((n, 4 * LANES), F32),
                 jax.ShapeDtypeStruct((n, 2 * LANES), F32),
                 jax.ShapeDtypeStruct((n, LANES), F32)]
    if rows_last:
        n_batch, seq = n // (n_time_tiles * tm), n_time_tiles * tm
        for width in (4 * LANES, 2 * LANES):
            out_specs.append(pl.BlockSpec((None, width, tm), lambda i: (i // n_time_tiles, 0, i % n_time_tiles)))
            out_shape.append(jax.ShapeDtypeStruct((n_batch, width, seq), F32))
    return pl.pallas_call(
        _nsa_prep_kernel,
        grid=(n // tm,),
        in_specs=[pl.BlockSpec((tm, MIXER_WIDTH), lambda i: (i, 0)),
                  pl.BlockSpec((tm, MIXER_WIDTH), lambda i: (i, 1)),
                  pl.BlockSpec((tm, LANES), lambda i: (i, (2 * MIXER_WIDTH + MEM_WIDTH) // LANES)),
                  tab, tab, tab, const(1, LANES), const(3, 1, LANES), const(LANES, LANES)],
        out_specs=out_specs,
        out_shape=out_shape,
        compiler_params=_params("parallel"),
        name="nsa_prep",
    )(u, u, u, cos, sa, sb, gq, gkv, seg)


def _compress_kernel(xk_ref, xv_ref, w_ref, o_ref):
    mb = o_ref.shape[0]
    for c, x_ref in enumerate((xk_ref, xv_ref)):
        acc = jnp.zeros((mb, LANES), F32)
        for r in range(NSA_BLOCK):
            xr = x_ref[pl.ds(r, mb, stride=NSA_BLOCK), :]
            acc = acc + jnp.dot(xr.astype(BF16), w_ref[c, r], preferred_element_type=F32)
        o_ref[:, c * LANES:(c + 1) * LANES] = acc


def compress(cs, wbd):
    nblk = cs.shape[0] // NSA_BLOCK
    mb = _pick_tile(nblk, CMP_BLOCKS)
    return pl.pallas_call(
        _compress_kernel,
        grid=(nblk // mb,),
        in_specs=[pl.BlockSpec((mb * NSA_BLOCK, LANES), lambda i: (i, 0)),
                  pl.BlockSpec((mb * NSA_BLOCK, LANES), lambda i: (i, 1)),
                  pl.BlockSpec(wbd.shape, lambda i: (0, 0, 0, 0))],
        out_specs=pl.BlockSpec((mb, 2 * LANES), lambda i: (i, 0)),
        out_shape=jax.ShapeDtypeStruct((nblk, 2 * LANES), F32),
        compiler_params=_params("parallel"),
        name="nsa_compress",
    )(cs, cs, wbd)


def _compress_pool_kernel(x_ref, w_ref, o_ref):
    mbp = o_ref.shape[0]
    low = _iota((1, LANES), 1) < HEAD_DIM
    for c in range(2):
        slabs = []
        for g in range(NSA_GROUPS):
            acc = jnp.zeros((mbp, LANES), F32)
            for d in range(HEAD_DIM):
                xr = x_ref[:, c * LANES + g * HEAD_DIM + d, :]
                acc = acc + jnp.dot(xr.astype(BF16), w_ref[c, d], preferred_element_type=F32)
            slabs.append(acc)
        first = jnp.where(low, slabs[0], pltpu.roll(slabs[1], HEAD_DIM, 1))
        second = jnp.where(low, pltpu.roll(slabs[0], HEAD_DIM, 1), slabs[1])
        o_ref[:, c * LANES:(c + 1) * LANES] = first
        o_ref[:, (2 + c) * LANES:(3 + c) * LANES] = second


def compress_pool(pool_t, layer, wt):
    n_pool = pool_t.shape[1]
    mbp = _pick_tile(n_pool, CMP_PAGES)
    return pl.pallas_call(
        _compress_pool_kernel,
        grid=(n_pool // mbp,),
        in_specs=[pl.BlockSpec((None, mbp, 2 * LANES, PAGE_SIZE), lambda i: (layer, i, 0, 0)),
                  pl.BlockSpec(wt.shape, lambda i: (0, 0, 0, 0))],
        out_specs=pl.BlockSpec((mbp, 4 * LANES), lambda i: (i, 0)),
        out_shape=jax.ShapeDtypeStruct((n_pool, 4 * LANES), F32),
        compiler_params=_params("parallel"),
        name="nsa_compress_pool",
    )(pool_t, wt)


def _group_queries(q_chunks, g):
    parts = []
    for h in range(NSA_HPG):
        hh = g * NSA_HPG + h
        chunk = q_chunks(hh // 2)
        if hh % 2 != g:
            chunk = pltpu.roll(chunk, HEAD_DIM, 1)
        parts.append(chunk)
    qz = jnp.concatenate(parts, axis=0) * SCORE_SCALE
    return jnp.where((_iota((1, LANES), 1) >> 6) == g, qz, 0.0)


def _select_blocks(imp, cur):
    r, nb = imp.shape
    nidx = _iota((1, nb), 1)
    score = jnp.where(nidx == cur, BIG, jnp.where(nidx < cur, imp, -1.0))
    rank = jnp.zeros((r, nb), F32)
    for m in range(nb):
        col = score[:, m:m + 1]
        rank = rank + jnp.where(nidx > m, jnp.where(col >= score, 1.0, 0.0), jnp.where(col > score, 1.0, 0.0))
    return (rank < float(NSA_TOPK)) & (nidx <= cur)


def _select_blocks_t(imp, cur, live_blocks, score_ref):
    nb, r = imp.shape
    nidx = _iota((nb, 1), 0)
    score = jnp.where(nidx == cur, BIG, jnp.where(nidx < cur, imp, -1.0))
    score_ref[...] = score

    def count_ahead(m, rank):
        row = score_ref[pl.ds(m, 1), :]
        return rank + jnp.where(nidx > m, jnp.where(row >= score, 1.0, 0.0), jnp.where(row > score, 1.0, 0.0))

    rank = lax.fori_loop(0, live_blocks, count_ahead, jnp.zeros((nb, r), F32))
    return jnp.where((rank < float(NSA_TOPK)) & (nidx <= cur), 1.0, 0.0)


def _masked_softmax_parts(s, ok):
    m = jnp.max(jnp.where(ok, s, -BIG), axis=-1, keepdims=True)
    p = jnp.where(ok, jnp.exp(s - m), 0.0)
    return p, jnp.maximum(jnp.sum(p, axis=-1, keepdims=True), 1e-30)


def _place_heads(res, g, rows):
    low = _iota((1, LANES), 1) < HEAD_DIM
    out = []
    for c in range(NSA_HPG // 2):
        a = res[(2 * c) * rows:(2 * c + 1) * rows]
        b = res[(2 * c + 1) * rows:(2 * c + 2) * rows]
        if g == 1:
            a = pltpu.roll(a, HEAD_DIM, 1)
        else:
            b = pltpu.roll(b, HEAD_DIM, 1)
        out.append(jnp.where(low, a, b))
    return out


def _nsa_attn_kernel(q_ref, g_ref, cs_ref, win_ref, cmp_ref, cmpt_ref, gexp_ref, o_ref, score_ref):
    qt = pl.program_id(1)
    t0 = qt * Q_TILE
    seq = cs_ref.shape[0]
    nb = cmp_ref.shape[0]
    tpos = t0 + _iota((Q_TILE, 1), 0)
    rows = NSA_HPG * Q_TILE

    gates = g_ref[...]
    gexp = [_dot_x2(gates, gexp_ref[j]) for j in range(3)]

    win_span = min(NSA_WINDOW + Q_TILE, seq)
    w0 = pl.multiple_of(jnp.maximum(t0 + Q_TILE - win_span, 0), Q_TILE)
    n_chunks = (t0 + Q_TILE + SEL_CHUNK - 1) // SEL_CHUNK

    t_lane = t0 + (_iota((1, rows), 1) & (Q_TILE - 1))
    block_row = _iota((nb, 1), 0)
    complete = (block_row + 1) * NSA_BLOCK <= t_lane + 1
    cur_lane = (t0 + _iota((1, Q_TILE), 1)) >> 6
    group_lane = _iota((1, LANES), 1) >> 6
    live_blocks = jnp.minimum((t0 + Q_TILE - 1) // NSA_BLOCK + 1, nb)

    qzb, sel_bf, o_cmp = [], [], []
    for g in range(NSA_GROUPS):
        qz = _group_queries(lambda c: q_ref[:, c * LANES:(c + 1) * LANES], g)
        qzb.append(qz.astype(BF16))
        s = _dot_nt_precise(cmp_ref[:, 0:LANES], qz)
        m_cmp = jnp.max(jnp.where(complete, s, -BIG), axis=0, keepdims=True)
        p = jnp.where(complete, jnp.exp(s - m_cmp), 0.0)
        p_cmp = p / jnp.maximum(jnp.sum(p, axis=0, keepdims=True), 1e-30)
        o_cmp_t = _dot(cmpt_ref[LANES:2 * LANES, :], p_cmp)
        o_cmp.append(jnp.concatenate([o_cmp_t[:, h * Q_TILE:(h + 1) * Q_TILE].T for h in range(NSA_HPG)], axis=0))
        importance = sum(p_cmp[:, h * Q_TILE:(h + 1) * Q_TILE] for h in range(NSA_HPG))
        sel_t = _select_blocks_t(importance, cur_lane, live_blocks, score_ref)
        sel = jnp.concatenate([sel_t, jnp.zeros((LANES - nb, Q_TILE), F32)], axis=0).T
        sel_bf.append(sel.astype(BF16))

    def with_ones(v, g):
        return jnp.where(group_lane == g, v, 1.0).astype(BF16)

    def normalised(res):
        return res / pltpu.roll(res, HEAD_DIM, 1)

    def sel_chunk(kc, carry):
        k0 = pl.multiple_of(kc * SEL_CHUNK, SEL_CHUNK)
        kch = cs_ref[pl.ds(k0, SEL_CHUNK), 2 * LANES:3 * LANES].astype(BF16)
        vch = cs_ref[pl.ds(k0, SEL_CHUNK), 3 * LANES:4 * LANES]
        kpos = k0 + _iota((1, SEL_CHUNK), 1)
        expand = jnp.where((kpos >> 6) == _iota((LANES, 1), 0), 1.0, 0.0).astype(BF16)
        causal = kpos <= tpos
        out = []
        for g, (m, acc) in enumerate(carry):
            chosen = jnp.dot(sel_bf[g], expand, preferred_element_type=F32)
            bias = jnp.where((chosen > 0.5) & causal, 0.0, -BIG)[None]
            sc = lax.dot_general(qzb[g], kch, NT_DIMS, preferred_element_type=F32
                                 ).reshape(NSA_HPG, Q_TILE, SEL_CHUNK) + bias
            m_new = jnp.maximum(m, jnp.max(sc, axis=-1, keepdims=True))
            pp = jnp.exp(sc - m_new).astype(BF16)
            pv = jnp.dot(pp.reshape(rows, SEL_CHUNK), with_ones(vch, g), preferred_element_type=F32)
            out.append((m_new, jnp.exp(m - m_new) * acc + pv.reshape(NSA_HPG, Q_TILE, LANES)))
        return tuple(out)

    init = tuple((jnp.full((NSA_HPG, Q_TILE, 1), -BIG, F32), jnp.zeros((NSA_HPG, Q_TILE, LANES), F32))
                 for _ in range(NSA_GROUPS))
    sel_state = lax.fori_loop(0, n_chunks, sel_chunk, init)

    kw = win_ref[pl.ds(w0, win_span), 0:LANES].astype(BF16)
    vw = win_ref[pl.ds(w0, win_span), LANES:2 * LANES]
    wpos = w0 + _iota((1, win_span), 1)
    w_bias = jnp.where((wpos <= tpos) & (wpos > tpos - NSA_WINDOW), 0.0, -BIG)[None]

    branch_out = []
    for g in range(NSA_GROUPS):
        o_slc = normalised(sel_state[g][1].reshape(rows, LANES))
        sw = lax.dot_general(qzb[g], kw, NT_DIMS, preferred_element_type=F32
                             ).reshape(NSA_HPG, Q_TILE, win_span) + w_bias
        pw = jnp.exp(sw - jnp.max(sw, axis=-1, keepdims=True)).astype(BF16)
        o_win = normalised(jnp.dot(pw.reshape(rows, win_span), with_ones(vw, g), preferred_element_type=F32))
        branch_out.append([_place_heads(o, g, Q_TILE) for o in (o_cmp[g], o_slc, o_win)])

    for c in range(MIXER_WIDTH // LANES):
        g, cc = divmod(c, NSA_HPG // 2)
        sl = slice(c * LANES, (c + 1) * LANES)
        o_ref[:, sl] = sum(gexp[j][:, sl] * branch_out[g][j][cc] for j in range(3))


def nsa_attn_prompt(q, gates, cs, win, cmp, gexp, n_batch):
    n = q.shape[0]
    t = n // n_batch
    nt = t // Q_TILE
    nb = cmp.shape[0] // n_batch
    assert nb <= LANES
    cmp_t = cmp.reshape(n_batch, nb, 2 * LANES).transpose(0, 2, 1)
    return pl.pallas_call(
        _nsa_attn_kernel,
        grid=(n_batch, nt),
        in_specs=[pl.BlockSpec((Q_TILE, MIXER_WIDTH), lambda b, i: (b * nt + i, 0)),
                  pl.BlockSpec((Q_TILE, LANES), lambda b, i: (b * nt + i, 0)),
                  pl.BlockSpec((t, 4 * LANES), lambda b, i: (b, 0)),
                  pl.BlockSpec((t, 2 * LANES), lambda b, i: (b, 0)),
                  pl.BlockSpec((nb, 2 * LANES), lambda b, i: (b, 0)),
                  pl.BlockSpec((None, 2 * LANES, nb), lambda b, i: (b, 0, 0)),
                  pl.BlockSpec(gexp.shape, lambda b, i: (0, 0, 0))],
        out_specs=pl.BlockSpec((Q_TILE, MIXER_WIDTH), lambda b, i: (b * nt + i, 0)),
        out_shape=jax.ShapeDtypeStruct((n, MIXER_WIDTH), F32),
        scratch_shapes=[pltpu.VMEM((nb, Q_TILE), F32)],
        compiler_params=_params("parallel", "parallel"),
        name="nsa_attn_prompt",
    )(q, gates, cs, win, cmp, cmp_t, gexp)


SAMPLE_ROWS = 16
NB_PAD = 64


def _sample_queries(q_row):
    row = _iota((SAMPLE_ROWS, 1), 0)
    lane_group = _iota((1, LANES), 1) >> 6
    qz = jnp.zeros((SAMPLE_ROWS, LANES), F32)
    for hh in range(NSA_HEADS):
        g = hh // NSA_HPG
        chunk = jnp.broadcast_to(q_row[:, (hh // 2) * LANES:(hh // 2 + 1) * LANES], (SAMPLE_ROWS, LANES))
        if hh % 2 != g:
            chunk = pltpu.roll(chunk, HEAD_DIM, 1)
        qz = qz + jnp.where((row == hh) & (lane_group == g), chunk, 0.0)
    return qz * SCORE_SCALE


def _row_to_column(row):
    n = row.shape[1]
    diag = _iota((n, 1), 0) == _iota((1, n), 1)
    return jnp.sum(jnp.where(diag, jnp.broadcast_to(row, (n, n)), 0.0), axis=1, keepdims=True)


def _attend_with_new_key(s, ok, qz, k_new, v_new, weighted_values):
    s_new = jnp.sum(_bf16_round(qz) * _bf16_round(k_new), axis=-1, keepdims=True)
    m_lanes = jnp.max(jnp.where(ok, s, -BIG), axis=-1, keepdims=True)
    if s.ndim == 3:
        m_lanes = jnp.max(m_lanes, axis=0)
    m = jnp.maximum(m_lanes, s_new)
    p = jnp.where(ok, jnp.exp(s - m), 0.0)
    p_new = jnp.exp(s_new - m)
    total = jnp.sum(p, axis=-1, keepdims=True)
    if s.ndim == 3:
        total = jnp.sum(total, axis=0)
    den = total + p_new
    return (weighted_values(p) + _bf16_round(p_new) * _bf16_round(v_new)) / den


def _nsa_sample_kernel(pt_ref, q_ref, cmp_ref, new_ref, win_ref, wnew_ref, g_ref, gexp_ref, *refs, past):
    n_pages = past // PAGE_SIZE
    n_seq = q_ref.shape[0]
    o_ref, wout_ref, cm_ref = refs[n_seq * n_pages:]
    for i in range(n_seq):
        _nsa_sample_one(pl.program_id(0) * n_seq + i, pt_ref, q_ref.at[i], cmp_ref, new_ref.at[i], win_ref.at[i],
                        wnew_ref.at[i], g_ref.at[i], gexp_ref, refs[i * n_pages:(i + 1) * n_pages], o_ref.at[i],
                        wout_ref.at[i], cm_ref.at[i], past)


def _nsa_sample_one(b, pt_ref, q_ref, cmp_ref, new_ref, win_ref, wnew_ref, g_ref, gexp_ref, page_refs, o_ref, wout_ref,
                    cm_ref, past):
    n_pages = past // PAGE_SIZE
    head = _iota((SAMPLE_ROWS, 1), 0)
    qz = _sample_queries(q_ref[...])
    qzb = qz.astype(BF16)

    cm_ref[...] = jnp.zeros(cm_ref.shape, F32)
    for p in range(n_pages):
        row = cmp_ref[pl.ds(pt_ref[b * n_pages + p], 1), :]
        cm_ref[2 * p:2 * p + 1, :] = row[:, 0:2 * LANES]
        cm_ref[2 * p + 1:2 * p + 2, :] = row[:, 2 * LANES:4 * LANES]
    s = _dot_nt_precise(qz, cm_ref[:, 0:LANES])
    nidx = _iota((1, NB_PAD), 1)
    complete = (nidx + 1) * NSA_BLOCK <= past + 1
    p, den = _masked_softmax_parts(s, complete)
    p_cmp = p / den
    o_cmp = _dot(p_cmp, cm_ref[:, LANES:2 * LANES])

    row8 = _iota((SUBLANES, 1), 0)
    imp = jnp.zeros((SUBLANES, NB_PAD), F32)
    for g in range(NSA_GROUPS):
        in_group = (head >= g * NSA_HPG) & (head < (g + 1) * NSA_HPG)
        imp = imp + jnp.where(row8 == g, jnp.sum(jnp.where(in_group, p_cmp, 0.0), axis=0, keepdims=True), 0.0)
    cur = jnp.full((SUBLANES, 1), past // NSA_BLOCK, jnp.int32)
    sel = jnp.where(_select_blocks(imp, cur), 1.0, 0.0)
    sel_rows = jnp.where(head < NSA_HPG, sel[0:1], sel[1:2]).astype(BF16)
    kpos = _iota((1, past), 1)
    expand = jnp.where((kpos >> 6) == _iota((NB_PAD, 1), 0), 1.0, 0.0).astype(BF16)
    chosen = jnp.dot(sel_rows, expand, preferred_element_type=F32)

    s3 = jnp.concatenate(
        [jnp.dot(qzb, page_refs[p][0:LANES, :].astype(BF16), preferred_element_type=F32) for p in range(n_pages)],
        axis=0).reshape(n_pages, SAMPLE_ROWS, PAGE_SIZE)
    ok3 = jnp.concatenate([chosen[:, p * PAGE_SIZE:(p + 1) * PAGE_SIZE] for p in range(n_pages)],
                          axis=0).reshape(n_pages, SAMPLE_ROWS, PAGE_SIZE) > 0.5

    def sel_values(pp):
        return sum(_dot_nt(pp[p], page_refs[p][LANES:2 * LANES, :]) for p in range(n_pages))

    o_slc = _attend_with_new_key(s3, ok3, qz, new_ref[:, 2 * LANES:3 * LANES], new_ref[:, 3 * LANES:4 * LANES],
                                 sel_values)

    lw = win_ref.shape[1]
    sw = jnp.dot(qzb, win_ref[0:LANES, :].astype(BF16), preferred_element_type=F32)
    okw = _iota((1, lw), 1) > lw - NSA_WINDOW
    o_win = _attend_with_new_key(sw, okw, qz, wnew_ref[:, 0:LANES], wnew_ref[:, LANES:2 * LANES],
                                 lambda pp: _dot_nt(pp, win_ref[LANES:2 * LANES, :]))

    gates = jnp.broadcast_to(g_ref[...], (SUBLANES, LANES))
    lane_head = _iota((1, MIXER_WIDTH), 1) >> 6
    in_place = (head & 1) == jnp.where(head >= NSA_HPG, 1, 0)
    out = jnp.zeros((1, MIXER_WIDTH), F32)
    for j, o in enumerate((o_cmp, o_slc, o_win)):
        tiled = jnp.concatenate([o] * (MIXER_WIDTH // LANES), axis=1)
        rolled = jnp.concatenate([pltpu.roll(o, HEAD_DIM, 1)] * (MIXER_WIDTH // LANES), axis=1)
        placed = jnp.where(lane_head == head, jnp.where(in_place, tiled, rolled), 0.0)
        out = out + _dot_x2(gates, gexp_ref[j])[0:1] * jnp.sum(placed, axis=0, keepdims=True)
    o_ref[...] = out

    new_col = jnp.broadcast_to(_row_to_column(wnew_ref[...]), (2 * LANES, LANES))
    last_lane = _iota((1, LANES), 1) == LANES - 1
    for c in range(lw // LANES):
        shifted = pltpu.roll(win_ref[:, c * LANES:(c + 1) * LANES], LANES - 1, 1)
        if (c + 1) * LANES < lw:
            carry_in = pltpu.roll(win_ref[:, (c + 1) * LANES:(c + 2) * LANES], LANES - 1, 1)
        else:
            carry_in = new_col
        wout_ref[:, c * LANES:(c + 1) * LANES] = jnp.where(last_lane, carry_in, shifted)


def nsa_sample(page_table, layer, q, cmp_pool, cs_new, win_t, win_new, gates, gexp, pool_t):
    s, n_pages = page_table.shape
    past = n_pages * PAGE_SIZE
    lw = win_t.shape[-1]
    sps = NSA_SEQS_PER_STEP
    assert s % sps == 0
    per_seq = lambda *shape: pl.BlockSpec((sps,) + shape, lambda b, pt: (b,) + (0,) * len(shape))
    page_spec = lambda i, p: pl.BlockSpec((None, None, 2 * LANES, PAGE_SIZE),
                                          lambda b, pt: (layer, pt[(b * sps + i) * n_pages + p], 1, 0))
    grid_spec = pltpu.PrefetchScalarGridSpec(
        num_scalar_prefetch=1,
        grid=(s // sps,),
        in_specs=[per_seq(1, MIXER_WIDTH),
                  pl.BlockSpec(cmp_pool.shape, lambda b, pt: (0, 0)),
                  per_seq(1, 4 * LANES),
                  pl.BlockSpec((None, sps, 2 * LANES, lw), lambda b, pt: (layer, b, 0, 0)),
                  per_seq(1, 2 * LANES), per_seq(1, LANES),
                  pl.BlockSpec(gexp.shape, lambda b, pt: (0, 0, 0))]
                 + [page_spec(i, p) for i in range(sps) for p in range(n_pages)],
        out_specs=[per_seq(1, MIXER_WIDTH), per_seq(2 * LANES, lw)],
        scratch_shapes=[pltpu.VMEM((sps, NB_PAD, 2 * LANES), F32)],
    )
    return pl.pallas_call(
        functools.partial(_nsa_sample_kernel, past=past),
        grid_spec=grid_spec,
        out_shape=[jax.ShapeDtypeStruct((s, 1, MIXER_WIDTH), F32),
                   jax.ShapeDtypeStruct((s, 2 * LANES, lw), F32)],
        compiler_params=_params("arbitrary"),
        name="nsa_sample",
    )(page_table.reshape(-1), q, cmp_pool, cs_new, win_t, win_new, gates, gexp, *([pool_t] * (sps * n_pages)))


def _layer_norm_swish(y, g, b):
    mu = jnp.mean(y, axis=-1, keepdims=True)
    var = jnp.mean(jnp.square(y - mu), axis=-1, keepdims=True)
    z = (y - mu) * lax.rsqrt(var + EPS) * g + b
    return z * _sigmoid(z)


def _conv_prompt_kernel(a_ref, b_ref, ap_ref, bp_ref, w_ref, cb_ref, lg_ref, lb_ref, o_ref, st_ref, buf_ref):
    i = pl.program_id(1)
    glu = a_ref[...] * _sigmoid(b_ref[...])
    prev = ap_ref[...] * _sigmoid(bp_ref[...])
    buf_ref[0:CONV_HALO, :] = jnp.where(i > 0, prev, 0.0)
    buf_ref[CONV_HALO:, :] = glu
    tt = a_ref.shape[0]
    acc = jnp.zeros(glu.shape, F32)
    for k in range(CONV_WIDTH):
        acc = acc + w_ref[k:k + 1, :] * buf_ref[pl.ds(CONV_HALO - (CONV_WIDTH - 1) + k, tt), :]
    o_ref[...] = _layer_norm_swish(acc + cb_ref[...], lg_ref[...], lb_ref[...])

    @pl.when(i == pl.num_programs(1) - 1)
    def _():
        st_ref[0] = glu[tt - CONV_HALO:, :]


def conv_prompt(u, w, cb, lg, lb, n_batch):
    n = u.shape[0]
    t = n // n_batch
    tt = min(CONV_TILE, t)
    nt = t // tt
    ch = MIXER_WIDTH
    ratio = tt // CONV_HALO
    cur = lambda col: pl.BlockSpec((tt, ch), lambda b, i: (b * nt + i, col))
    prev = lambda col: pl.BlockSpec((CONV_HALO, ch), lambda b, i: (jnp.maximum((b * nt + i) * ratio - 1, 0), col))
    const = lambda *shape: pl.BlockSpec(shape, lambda b, i: (0,) * len(shape))
    return pl.pallas_call(
        _conv_prompt_kernel,
        grid=(n_batch, nt),
        in_specs=[cur(0), cur(1), prev(0), prev(1), const(CONV_HALO, ch), const(1, ch), const(1, ch), const(1, ch)],
        out_specs=[pl.BlockSpec((tt, ch), lambda b, i: (b * nt + i, 0)),
                   pl.BlockSpec((1, CONV_HALO, ch), lambda b, i: (b, 0, 0))],
        out_shape=[jax.ShapeDtypeStruct((n, ch), F32),
                   jax.ShapeDtypeStruct((n_batch, CONV_HALO, ch), F32)],
        scratch_shapes=[pltpu.VMEM((CONV_HALO + tt, ch), F32)],
        compiler_params=_params("parallel", "arbitrary"),
        name="conv_prompt",
    )(u, u, u, u, w, cb, lg, lb)


def _conv_sample_kernel(a_ref, b_ref, st_ref, w_ref, cb_ref, lg_ref, lb_ref, o_ref, glu_ref):
    glu = a_ref[...] * _sigmoid(b_ref[...])
    acc = w_ref[CONV_WIDTH - 1:CONV_WIDTH, :] * glu
    for k in range(CONV_WIDTH - 1):
        acc = acc + w_ref[k:k + 1, :] * st_ref[k]
    o_ref[...] = _layer_norm_swish(acc + cb_ref[...], lg_ref[...], lb_ref[...])
    glu_ref[...] = glu


def conv_sample(u, state_t, layer, w, cb, lg, lb):
    s = u.shape[0]
    ch = MIXER_WIDTH
    const = lambda *shape: pl.BlockSpec(shape, lambda i: (0,) * len(shape))
    return pl.pallas_call(
        _conv_sample_kernel,
        grid=(1,),
        in_specs=[pl.BlockSpec((s, ch), lambda i: (0, 0)), pl.BlockSpec((s, ch), lambda i: (0, 1)),
                  pl.BlockSpec((None, CONV_WIDTH - 1, s, ch), lambda i: (layer, 0, 0, 0)),
                  const(CONV_HALO, ch), const(1, ch), const(1, ch), const(1, ch)],
        out_specs=[const(s, ch), const(s, ch)],
        out_shape=[jax.ShapeDtypeStruct((s, ch), F32), jax.ShapeDtypeStruct((s, ch), F32)],
        compiler_params=_params("arbitrary"),
        name="conv_sample",
    )(u, u, state_t, w, cb, lg, lb)


SB_PAIRS = MIXER_WIDTH // LANES


def _stick_weights(z, valid, later, tri):
    sp = _softplus(z)
    log_stay = jnp.where(valid, -sp, 0.0)
    within = _dot_x2(log_stay, tri)
    a = jnp.where(valid, jnp.exp(z - sp + within + later), 0.0)
    return a, within[:, 0:1] + log_stay[:, 0:1]


def _sb_attn_kernel(q_ref, k_ref, v_ref, tri_ref, o_ref, acc_ref):
    qt = pl.program_id(1)
    t0 = qt * Q_TILE
    low = _iota((1, LANES), 1) < HEAD_DIM
    pair_rows = 2 * Q_TILE
    tpos = t0 + (_iota((SB_PAIRS * pair_rows, 1), 0) & (Q_TILE - 1))
    tri = tri_ref[...]
    qz = []
    for pr in range(SB_PAIRS):
        q2 = q_ref[:, pr * LANES:(pr + 1) * LANES] * SCORE_SCALE
        qz.append(jnp.concatenate([jnp.where(low, q2, 0.0), jnp.where(low, 0.0, q2)], axis=0).astype(BF16))
    acc_ref[...] = jnp.zeros(acc_ref.shape, F32)

    def alive(carry):
        return (carry[0] <= qt) & (carry[1] > 0)

    def chunk(carry):
        i, _, later = carry
        k0 = pl.multiple_of((qt - i) * SB_CHUNK, SB_CHUNK)
        valid = (k0 + _iota((1, SB_CHUNK), 1)) < tpos
        z = jnp.concatenate(
            [lax.dot_general(qz[pr], k_ref[pl.ds(k0, SB_CHUNK), pr * LANES:(pr + 1) * LANES], NT_DIMS,
                             preferred_element_type=F32) for pr in range(SB_PAIRS)], axis=0)
        a, total = _stick_weights(z, valid, later, tri)
        a = a.astype(BF16)
        for pr in range(SB_PAIRS):
            acc_ref[pr] += jnp.dot(a[pr * pair_rows:(pr + 1) * pair_rows],
                                   v_ref[pl.ds(k0, SB_CHUNK), pr * LANES:(pr + 1) * LANES],
                                   preferred_element_type=F32)
        later = later + total
        live = jnp.max(later) > UNDERFLOW_LOG
        return i + 1, live.astype(jnp.int32), later

    init = (jnp.int32(0), jnp.int32(1), jnp.zeros((SB_PAIRS * pair_rows, 1), F32))
    lax.while_loop(alive, chunk, init)
    for pr in range(SB_PAIRS):
        o_ref[:, pr * LANES:(pr + 1) * LANES] = jnp.where(low, acc_ref[pr, :Q_TILE], acc_ref[pr, Q_TILE:])


def sb_attn_prompt(u, kv16, tri, n_batch):
    n = u.shape[0]
    t = n // n_batch
    nt = t // Q_TILE
    return pl.pallas_call(
        _sb_attn_kernel,
        grid=(n_batch, nt),
        in_specs=[pl.BlockSpec((Q_TILE, MIXER_WIDTH), lambda b, i: (b * nt + i, 0)),
                  pl.BlockSpec((t, MIXER_WIDTH), lambda b, i: (b, 0)),
                  pl.BlockSpec((t, MIXER_WIDTH), lambda b, i: (b, 1)),
                  pl.BlockSpec(tri.shape, lambda b, i: (0, 0))],
        out_specs=pl.BlockSpec((Q_TILE, MIXER_WIDTH), lambda b, i: (b * nt + i, 0)),
        out_shape=jax.ShapeDtypeStruct((n, MIXER_WIDTH), F32),
        scratch_shapes=[pltpu.VMEM((SB_PAIRS, 2 * Q_TILE, LANES), F32)],
        compiler_params=_params("parallel", "parallel"),
        name="sb_attn_prompt",
    )(u, kv16, kv16, tri)


def _sb_sample_kernel(pt_ref, q_ref, tri_ref, suffix_ref, *refs):
    page_refs, o_ref = refs[:-1], refs[-1]
    n_pages = len(page_refs)
    head = _iota((SAMPLE_ROWS, 1), 0)
    lane_head = _iota((1, MIXER_WIDTH), 1) >> 6
    qz = (jnp.where(lane_head == head, jnp.broadcast_to(q_ref[0], (SAMPLE_ROWS, MIXER_WIDTH)), 0.0)
          * SCORE_SCALE).astype(BF16)
    z = jnp.concatenate(
        [jnp.dot(qz, page_refs[p][0:MIXER_WIDTH, :].astype(BF16), preferred_element_type=F32) for p in range(n_pages)],
        axis=0)
    sp = _softplus(z)
    log_stay = -sp
    within = _dot_x3(log_stay, tri_ref[...])
    total = jnp.broadcast_to(within[:, 0:1] + log_stay[:, 0:1], z.shape)
    later = _dot_w3(suffix_ref[...], total)
    a = jnp.exp(z - sp + within + later)
    acc = sum(_dot_nt(a[p * SAMPLE_ROWS:(p + 1) * SAMPLE_ROWS], page_refs[p][MIXER_WIDTH:2 * MIXER_WIDTH, :])
              for p in range(n_pages))
    o_ref[0] = jnp.sum(jnp.where(lane_head == head, acc, 0.0), axis=0, keepdims=True)


def sb_attn_sample(page_table, layer, q, pool_t, tri, suffix):
    s, n_pages = page_table.shape
    page_spec = lambda p: pl.BlockSpec((None, None, 2 * MIXER_WIDTH, PAGE_SIZE),
                                       lambda b, pt: (layer, pt[b * n_pages + p], 0, 0))
    grid_spec = pltpu.PrefetchScalarGridSpec(
        num_scalar_prefetch=1,
        grid=(s,),
        in_specs=[pl.BlockSpec((1, 1, MIXER_WIDTH), lambda b, pt: (b, 0, 0)),
                  pl.BlockSpec(tri.shape, lambda b, pt: (0, 0)),
                  pl.BlockSpec(suffix.shape, lambda b, pt: (0, 0))]
                 + [page_spec(p) for p in range(n_pages)],
        out_specs=pl.BlockSpec((1, 1, MIXER_WIDTH), lambda b, pt: (b, 0, 0)),
    )
    return pl.pallas_call(
        _sb_sample_kernel,
        grid_spec=grid_spec,
        out_shape=jax.ShapeDtypeStruct((s, 1, MIXER_WIDTH), F32),
        compiler_params=_params("arbitrary"),
        name="sb_attn_sample",
    )(page_table.reshape(-1), q, tri, suffix, *([pool_t] * n_pages))


def _rope_tables(positions):
    half = ROT_DIM // 2
    inv = np.exp(-math.log(ROPE_THETA) * np.arange(0, ROT_DIM, 2, dtype=np.float64) / ROT_DIM)
    ang = np.asarray(positions, np.float64)[:, None] * inv[None, :]
    lane = np.arange(LANES) % HEAD_DIM
    cos = np.ones((len(positions), LANES))
    sa = np.zeros((len(positions), LANES))
    sb = np.zeros((len(positions), LANES))
    for l in range(LANES):
        r = lane[l]
        if r < half:
            cos[:, l] = np.cos(ang[:, r])
            sb[:, l] = -np.sin(ang[:, r])
        elif r < ROT_DIM:
            cos[:, l] = np.cos(ang[:, r - half])
            sa[:, l] = np.sin(ang[:, r - half])
    return tuple(jnp.asarray(a, F32) for a in (cos, sa, sb))


def _segment_ones():
    lane = np.arange(LANES)
    return jnp.asarray(lane[:, None] // HEAD_DIM == lane[None, :] // HEAD_DIM, BF16)


def _gate_expand():
    e = np.zeros((3, LANES, MIXER_WIDTH), np.float32)
    for j in range(3):
        for h in range(NSA_HEADS):
            e[j, 3 * h + j, h * HEAD_DIM:(h + 1) * HEAD_DIM] = 1.0
    return jnp.asarray(e, BF16)


def _later_ones(n):
    idx = np.arange(n)
    return jnp.asarray(idx[:, None] > idx[None, :], BF16)


def _later_pages(n_pages):
    p = np.repeat(np.arange(n_pages), SAMPLE_ROWS)
    r = np.tile(np.arange(SAMPLE_ROWS), n_pages)
    return jnp.asarray((r[:, None] == r[None, :]) & (p[None, :] > p[:, None]), BF16)


def _block_diag2(w):
    z = jnp.zeros_like(w)
    return jnp.concatenate([jnp.concatenate([w, z], axis=-1), jnp.concatenate([z, w], axis=-1)], axis=-2)


def _compress_weights(w_cmp):
    return _block_diag2(w_cmp.reshape(2, NSA_BLOCK, HEAD_DIM, HEAD_DIM).astype(BF16))


def _compress_weights_t(w_cmp):
    assert PAGE_SIZE == 2 * NSA_BLOCK
    w = w_cmp.reshape(2, NSA_BLOCK, HEAD_DIM, HEAD_DIM).transpose(0, 2, 1, 3)
    return _block_diag2(w.astype(BF16))


def _rows_last(x):
    a, b, rows = x.shape[:3]
    perm = (0, 1) + tuple(range(3, x.ndim)) + (2,)
    return x.transpose(perm).reshape(a, b, -1, rows)


def kernel(x_prompt, x_sample, cache_nsa_kv, state_nsa_win, state_conv, cache_sb_kv, cache_mem_kv, page_table,
           mem_prompt, norm_mix, norm_ffn, norm_mem, w_in_nsa, w_in_conv, w_in_sb, w_out, w_mem_kv, qk_norm_nsa,
           qk_norm_mem, w_nsa_cmp, conv_w, conv_b, conv_ln_g, conv_ln_b, w_ffn_in, w_ffn_out):
    n_batch, seq, d_model = x_prompt.shape
    n_dec = x_sample.shape[0]
    depth = w_out.shape[0]
    n_pages = page_table.shape[1]
    past = n_pages * PAGE_SIZE
    n_mem = mem_prompt.shape[1]
    ffn_hidden = w_ffn_out.shape[1]
    assert x_sample.shape[1] == 1 and seq % CONV_TILE == 0 and seq >= NSA_WINDOW + Q_TILE
    assert past % NSA_BLOCK == 0 and ffn_hidden % LANES == 0 and n_dec % MEM_BATCH == 0
    assert PAGE_SIZE == SB_CHUNK == LANES and state_nsa_win.shape[2] % LANES == 0

    xp = x_prompt.reshape(n_batch * seq, d_model)
    xs = x_sample.reshape(n_dec, d_model)

    seg = _segment_ones()
    gexp = _gate_expand()
    tri = _later_ones(SB_CHUNK)
    suffix = _later_pages(n_pages)
    rope_p = _rope_tables(np.arange(seq))
    rope_s = _rope_tables(np.full((n_dec,), past))

    nsa_pool_t = _rows_last(cache_nsa_kv)
    nsa_win_t = _rows_last(state_nsa_win)
    sb_pool_t = _rows_last(cache_sb_kv)
    mem_cache_t = _rows_last(cache_mem_kv)
    conv_state_t = state_conv.transpose(0, 2, 1, 3)

    nq, nkv = MIXER_WIDTH, 6 * NSA_GROUPS * HEAD_DIM
    n_gate = 3 * NSA_HEADS
    w_nsa = jnp.concatenate(
        [w_in_nsa[:, :, :nq + nkv], w_in_nsa[:, :, nq + nkv + n_gate:], w_in_nsa[:, :, nq + nkv:nq + nkv + n_gate],
         jnp.zeros(w_in_nsa.shape[:2] + (LANES - n_gate,), F32)], axis=-1).astype(BF16)
    w_conv = w_in_conv.astype(BF16)
    w_sb = w_in_sb.astype(BF16)
    w_o1 = w_out[:, :MIXER_WIDTH].astype(BF16)
    w_o2 = w_out[:, MIXER_WIDTH:].astype(BF16)
    w_gate_up = w_ffn_in.astype(BF16)
    w_down = w_ffn_out.astype(BF16)

    gq_mem = jnp.tile(qk_norm_mem[:, 0], (1, LANES // HEAD_DIM))
    gk_mem = jnp.tile(qk_norm_mem[:, 1], (1, LANES // HEAD_DIM))
    mem_kv = mem_kv_all(mem_prompt.reshape(n_batch * n_mem, d_model), norm_mem, w_mem_kv.astype(BF16), gk_mem, seg)

    nsa_p, nsa_s, win_p, win_s, conv_p, conv_s, sb_p, sb_s = [], [], [], [], [], [], [], []
    for layer in range(depth):
        kind, j = layer % N_MIXERS, layer // N_MIXERS
        if kind == 0:
            up = proj_in(xp, norm_mix[layer], w_nsa[j])
            us = proj_in(xs, norm_mix[layer], w_nsa[j])
            mem_col = 2 * MIXER_WIDTH // MEM_WIDTH
            gq = jnp.tile(qk_norm_nsa[j, 0], LANES // HEAD_DIM).reshape(1, LANES)
            gkv = jnp.tile(qk_norm_nsa[j, 1:4], (1, LANES // HEAD_DIM)).reshape(3, 1, LANES)

            q_p, cs_p, wn_p, gt_p, cs_pt, wn_pt = nsa_prep(up, rope_p, gq, gkv, seg, seq // 256, 256, rows_last=True)
            cmp_p = compress(cs_p, _compress_weights(w_nsa_cmp[j]))
            o_p = nsa_attn_prompt(q_p, gt_p, cs_p, wn_p, cmp_p, gexp, n_batch)

            q_s, cs_s, wn_s, gt_s = nsa_prep(us, rope_s, gq, gkv, seg, 1, n_dec)
            cmp_pool = compress_pool(nsa_pool_t, j, _compress_weights_t(w_nsa_cmp[j]))
            o_s, win_new_t = nsa_sample(page_table, j, q_s.reshape(n_dec, 1, MIXER_WIDTH), cmp_pool,
                                        cs_s.reshape(n_dec, 1, 4 * LANES), nsa_win_t,
                                        wn_s.reshape(n_dec, 1, 2 * LANES), gt_s.reshape(n_dec, 1, LANES), gexp,
                                        nsa_pool_t)
            o_s = o_s.reshape(n_dec, MIXER_WIDTH)

            lw = nsa_win_t.shape[-1]
            keep = min(NSA_WINDOW, seq)
            nsa_p.append(cs_pt.reshape(n_batch, 4, NSA_GROUPS, HEAD_DIM, seq).transpose(0, 4, 1, 2, 3))
            nsa_s.append(cs_s.reshape(n_dec, 1, 4, NSA_GROUPS, HEAD_DIM))
            win_p.append(wn_pt[:, :, seq - keep:].reshape(n_batch, 2, NSA_GROUPS, HEAD_DIM, keep).transpose(0, 4, 1, 2, 3))
            win_s.append(win_new_t.reshape(n_dec, 2, NSA_GROUPS, HEAD_DIM, lw).transpose(0, 4, 1, 2, 3))
        elif kind == 1:
            up = proj_in(xp, norm_mix[layer], w_conv[j])
            us = proj_in(xs, norm_mix[layer], w_conv[j])
            mem_col = 2 * MIXER_WIDTH // MEM_WIDTH
            cw = jnp.concatenate([conv_w[j], jnp.zeros((CONV_HALO - CONV_WIDTH, MIXER_WIDTH), F32)], axis=0)
            vecs = [v[j].reshape(1, MIXER_WIDTH) for v in (conv_b, conv_ln_g, conv_ln_b)]
            o_p, tail = conv_prompt(up, cw, *vecs, n_batch)
            o_s, glu_s = conv_sample(us, conv_state_t, j, cw, *vecs)
            conv_p.append(tail[:, CONV_HALO - (CONV_WIDTH - 1):])
            conv_s.append(jnp.concatenate([conv_state_t[j, 1:], glu_s[None]], axis=0).transpose(1, 0, 2))
        else:
            up, kv16, kv_t = proj_in(xp, norm_mix[layer], w_sb[j], narrow_cols=(MIXER_WIDTH, 3 * MIXER_WIDTH),
                                     n_batch=n_batch)
            us = proj_in(xs, norm_mix[layer], w_sb[j])
            mem_col = 3 * MIXER_WIDTH // MEM_WIDTH
            o_p = sb_attn_prompt(up, kv16, tri, n_batch)
            o_s = sb_attn_sample(page_table, j, us[:, :MIXER_WIDTH].reshape(n_dec, 1, MIXER_WIDTH), sb_pool_t, tri,
                                 suffix).reshape(n_dec, MIXER_WIDTH)
            sb_p.append(kv_t.reshape(n_batch, 2, NSA_HEADS, HEAD_DIM, seq).transpose(0, 4, 1, 2, 3))
            sb_s.append(us[:, MIXER_WIDTH:3 * MIXER_WIDTH].reshape(n_dec, 1, 2, NSA_HEADS, HEAD_DIM))

        gq_l = gq_mem[layer].reshape(1, LANES)
        om_p = mem_attn_prompt(up, mem_col, mem_kv[layer], gq_l, seg, n_batch)
        om_s = mem_attn_sample(us[:, mem_col * MEM_WIDTH:(mem_col + 1) * MEM_WIDTH], mem_cache_t, layer, gq_l, seg)

        ffn_w = (w_o1[layer], w_o2[layer], norm_ffn[layer], w_gate_up[layer], w_down[layer])
        xp = out_ffn(o_p, om_p, xp, *ffn_w)
        xs = out_ffn(o_s, om_s, xs, *ffn_w)

    return (xp.reshape(x_prompt.shape), xs.reshape(x_sample.shape),
            jnp.stack(nsa_p), jnp.stack(nsa_s), jnp.stack(win_p), jnp.stack(win_s),
            jnp.stack(conv_p), jnp.stack(conv_s), jnp.stack(sb_p), jnp.stack(sb_s),
            mem_kv.reshape(depth, n_batch, n_mem, 2, MEM_HEADS, HEAD_DIM))
```
